```python
import jax, jax.numpy as jnp
from jax import lax
import numpy as np

D_MODEL = 2048
BATCH = 1
SEQ = 8192
DEPTH = 1
DEC_BATCH = 128
DEC_SEQ = 8
PAST_LEN = 2048
PAGE_SIZE = 128

SB_HEAD_DIM = 64
SB_WIDTH = D_MODEL // 2
SB_HEADS = SB_WIDTH // SB_HEAD_DIM
SB_BIAS_INIT = -6.0
CONV_WIDTH = D_MODEL // 2
CONV_KERNEL = 31
MEM_TOKENS = 256
MEM_HEADS = 4
MEM_HEAD_DIM = D_MODEL // 8
MEM_WIDTH = MEM_HEADS * MEM_HEAD_DIM
N_BRANCH = 3
Q_BLOCK = 128
EPS = 1e-6
IN_COLS = 4 * SB_WIDTH + 3 * CONV_WIDTH + 2 * MEM_WIDTH + N_BRANCH * D_MODEL

kernel_name = 'stickbreak_conformer_memory_hybrid_step'


def rmsnorm(x, g):
    xf = x.astype(jnp.float32)
    r = lax.rsqrt(jnp.mean(xf * xf, axis=-1, keepdims=True) + EPS)
    return (xf * r * g.astype(jnp.float32)).astype(x.dtype)


def layernorm(x, g, b):
    xf = x.astype(jnp.float32)
    mu = jnp.mean(xf, axis=-1, keepdims=True)
    var = jnp.mean(jnp.square(xf - mu), axis=-1, keepdims=True)
    y = (xf - mu) * lax.rsqrt(var + EPS) * g.astype(jnp.float32) + b.astype(jnp.float32)
    return y.astype(x.dtype)


def split_cols(z):
    sizes = [SB_WIDTH] * 4 + [CONV_WIDTH] * 3 + [MEM_WIDTH] * 2 + [D_MODEL] * N_BRANCH
    idx = np.cumsum(sizes)[:-1].tolist()
    return jnp.split(z, idx, axis=-1)


def sb_block(q, k, v, bias, q_pos, k_pos):
    z = jnp.einsum('bqhd,bkhd->bhqk', q.astype(jnp.float32), k.astype(jnp.float32)) * (SB_HEAD_DIM ** -0.5)
    z = z + bias.astype(jnp.float32)[None, :, None, None]
    mask = k_pos[None, :] < q_pos[:, None]
    log_fail = jnp.where(mask, jax.nn.log_sigmoid(-z), 0.0)
    suffix = lax.cumsum(log_fail, axis=3, reverse=True) - log_fail
    w = jnp.where(mask, jnp.exp(jax.nn.log_sigmoid(z) + suffix), 0.0)
    o = jnp.einsum('bhqk,bkhd->bqhd', w, v.astype(jnp.float32))
    return o.astype(q.dtype)


def sb_attention(q, k, v, bias, past):
    tq = q.shape[1]
    outs = []
    for start in range(0, tq, Q_BLOCK):
        end = min(start + Q_BLOCK, tq)
        n_keys = past + end
        q_pos = past + jnp.arange(start, end, dtype=jnp.int32)
        k_pos = jnp.arange(n_keys, dtype=jnp.int32)
        outs.append(sb_block(q[:, start:end], k[:, :n_keys], v[:, :n_keys], bias, q_pos, k_pos))
    return jnp.concatenate(outs, axis=1)


def causal_dwconv(u_ext, w, b):
    c = u_ext.shape[-1]
    out = lax.conv_general_dilated(u_ext, w.astype(u_ext.dtype)[:, None, :], window_strides=(1,),
                                   padding='VALID', dimension_numbers=('NWC', 'WIO', 'NWC'),
                                   feature_group_count=c)
    return out + b


def memory_kv(mem, g, w_kv):
    b, m, _ = mem.shape
    kv = rmsnorm(mem, g) @ w_kv
    mk, mv = jnp.split(kv, 2, axis=-1)
    return (mk.reshape(b, m, MEM_HEADS, MEM_HEAD_DIM), mv.reshape(b, m, MEM_HEADS, MEM_HEAD_DIM))


def layer_forward(x, conv_hist, k_past, v_past, mem_k, mem_v, p):
    b, t, _ = x.shape
    h = rmsnorm(x, p['norm_in'])
    z = h @ p['w_in']
    (q, k, v, g_sb, glu_a, glu_b, g_cv, q_m, g_m, m_sb, m_cv, m_m) = split_cols(z)

    q = q.reshape(b, t, SB_HEADS, SB_HEAD_DIM)
    k = k.reshape(b, t, SB_HEADS, SB_HEAD_DIM)
    v = v.reshape(b, t, SB_HEADS, SB_HEAD_DIM)
    if k_past is None:
        past = 0
        k_all, v_all = k, v
    else:
        past = k_past.shape[1]
        k_all = jnp.concatenate([k_past.astype(k.dtype), k], axis=1)
        v_all = jnp.concatenate([v_past.astype(v.dtype), v], axis=1)
    o_sb = sb_attention(q, k_all, v_all, p['sb_bias'], past).reshape(b, t, SB_WIDTH)

    u = glu_a * jax.nn.sigmoid(glu_b)
    u_ext = jnp.concatenate([conv_hist.astype(u.dtype), u], axis=1)
    c = causal_dwconv(u_ext, p['conv_w'], p['conv_b'])
    o_cv = jax.nn.silu(layernorm(c, p['ln_g'], p['ln_b']))
    new_hist = u_ext[:, -(CONV_KERNEL - 1):]

    qm = q_m.reshape(b, t, MEM_HEADS, MEM_HEAD_DIM)
    s = jnp.einsum('bqhd,bmhd->bhqm', qm.astype(jnp.float32), mem_k.astype(jnp.float32)) * (MEM_HEAD_DIM ** -0.5)
    pm = jax.nn.softmax(s, axis=-1)
    o_m = jnp.einsum('bhqm,bmhd->bqhd', pm, mem_v.astype(jnp.float32)).astype(x.dtype).reshape(b, t, MEM_WIDTH)

    p_sb = (o_sb * jax.nn.silu(g_sb)) @ p['w_proj_sb']
    p_cv = (o_cv * jax.nn.silu(g_cv)) @ p['w_proj_cv']
    p_m = (o_m * jax.nn.silu(g_m)) @ p['w_proj_mem']
    mb = p['merge_bias']
    merged = (jax.nn.sigmoid(m_sb + mb[0]) * p_sb + jax.nn.sigmoid(m_cv + mb[1]) * p_cv
              + jax.nn.sigmoid(m_m + mb[2]) * p_m)
    y = x + merged @ p['w_out']
    return y, k, v, new_hist


def setup_inputs(seed: int = 0) -> dict:
    key = jax.random.key(seed)
    ks = jax.random.split(key, 24)
    n_pages = PAST_LEN // PAGE_SIZE
    n_used = DEC_BATCH * n_pages
    n_phys = n_used + max(1, n_used // 4)
    f32 = jnp.float32
    nrm = lambda k, shape, s: jax.random.normal(k, shape, f32) * s
    page_table = jax.random.permutation(ks[0], n_phys)[:n_used].reshape(DEC_BATCH, n_pages).astype(jnp.int32)
    return {
        'x_prompt': nrm(ks[1], (BATCH, SEQ, D_MODEL), 1.0),
        'x_sample': nrm(ks[2], (DEC_BATCH, DEC_SEQ, D_MODEL), 1.0),
        'mem_prompt': nrm(ks[3], (BATCH, MEM_TOKENS, D_MODEL), 1.0),
        'cache_k': nrm(ks[4], (DEPTH, n_phys, PAGE_SIZE, SB_HEADS, SB_HEAD_DIM), 1.0),
        'cache_v': nrm(ks[5], (DEPTH, n_phys, PAGE_SIZE, SB_HEADS, SB_HEAD_DIM), 1.0),
        'cache_conv': nrm(ks[6], (DEPTH, DEC_BATCH, CONV_KERNEL - 1, CONV_WIDTH), 0.5),
        'cache_mem_k': nrm(ks[7], (DEPTH, DEC_BATCH, MEM_TOKENS, MEM_HEADS, MEM_HEAD_DIM), 1.0),
        'cache_mem_v': nrm(ks[8], (DEPTH, DEC_BATCH, MEM_TOKENS, MEM_HEADS, MEM_HEAD_DIM), 1.0),
        'page_table': page_table,
        'norm_in': 1.0 + nrm(ks[9], (DEPTH, D_MODEL), 0.02),
        'w_in': nrm(ks[10], (DEPTH, D_MODEL, IN_COLS), D_MODEL ** -0.5),
        'sb_bias': SB_BIAS_INIT + nrm(ks[23], (DEPTH, SB_HEADS), 0.1),
        'merge_bias': nrm(ks[11], (DEPTH, N_BRANCH, D_MODEL), 0.02),
        'conv_w': nrm(ks[12], (DEPTH, CONV_KERNEL, CONV_WIDTH), CONV_KERNEL ** -0.5),
        'conv_b': nrm(ks[13], (DEPTH, CONV_WIDTH), 0.02),
        'ln_g': 1.0 + nrm(ks[14], (DEPTH, CONV_WIDTH), 0.02),
        'ln_b': nrm(ks[15], (DEPTH, CONV_WIDTH), 0.02),
        'norm_mem': 1.0 + nrm(ks[16], (DEPTH, D_MODEL), 0.02),
        'w_mem_kv': nrm(ks[17], (DEPTH, D_MODEL, 2 * MEM_WIDTH), D_MODEL ** -0.5),
        'w_proj_sb': nrm(ks[18], (DEPTH, SB_WIDTH, D_MODEL), SB_WIDTH ** -0.5),
        'w_proj_cv': nrm(ks[19], (DEPTH, CONV_WIDTH, D_MODEL), CONV_WIDTH ** -0.5),
        'w_proj_mem': nrm(ks[20], (DEPTH, MEM_WIDTH, D_MODEL), MEM_WIDTH ** -0.5),
        'w_out': nrm(ks[21], (DEPTH, D_MODEL, D_MODEL), D_MODEL ** -0.5),
        'norm_final': 1.0 + nrm(ks[22], (D_MODEL,), 0.02),
    }


def reference(x_prompt, x_sample, mem_prompt, cache_k, cache_v, cache_conv, cache_mem_k, cache_mem_v,
              page_table, norm_in, w_in, sb_bias, merge_bias, conv_w, conv_b, ln_g, ln_b, norm_mem, w_mem_kv,
              w_proj_sb, w_proj_cv, w_proj_mem, w_out, norm_final):
    bp = x_prompt.shape[0]
    bd = x_sample.shape[0]
    past = page_table.shape[1] * cache_k.shape[2]
    yp, ys = x_prompt, x_sample
    kp_l, vp_l, cp_l, mkp_l, mvp_l, ks_l, vs_l, cs_l = [], [], [], [], [], [], [], []
    for l in range(DEPTH):
        p = {'norm_in': norm_in[l], 'w_in': w_in[l], 'sb_bias': sb_bias[l], 'merge_bias': merge_bias[l],
             'conv_w': conv_w[l], 'conv_b': conv_b[l], 'ln_g': ln_g[l], 'ln_b': ln_b[l],
             'w_proj_sb': w_proj_sb[l], 'w_proj_cv': w_proj_cv[l], 'w_proj_mem': w_proj_mem[l],
             'w_out': w_out[l]}
        mk, mv = memory_kv(mem_prompt, norm_mem[l], w_mem_kv[l])
        hist0 = jnp.zeros((bp, CONV_KERNEL - 1, CONV_WIDTH), yp.dtype)
        yp, kp, vp, cp = layer_forward(yp, hist0, None, None, mk, mv, p)
        kp_l.append(kp); vp_l.append(vp); cp_l.append(cp); mkp_l.append(mk); mvp_l.append(mv)
        k_past = cache_k[l][page_table].reshape(bd, past, SB_HEADS, SB_HEAD_DIM)
        v_past = cache_v[l][page_table].reshape(bd, past, SB_HEADS, SB_HEAD_DIM)
        ys, k_s, v_s, c_s = layer_forward(ys, cache_conv[l], k_past, v_past, cache_mem_k[l], cache_mem_v[l], p)
        ks_l.append(k_s); vs_l.append(v_s); cs_l.append(c_s)
    y_prompt = rmsnorm(yp, norm_final)
    y_sample = rmsnorm(ys, norm_final)
    return (y_prompt, y_sample, jnp.stack(kp_l), jnp.stack(vp_l), jnp.stack(cp_l), jnp.stack(mkp_l),
            jnp.stack(mvp_l), jnp.stack(ks_l), jnp.stack(vs_l), jnp.stack(cs_l))
```

```python
import functools

import jax
import jax.numpy as jnp
import numpy as np
from jax import lax
from jax.experimental import pallas as pl
from jax.experimental.pallas import tpu as pltpu

F32 = jnp.float32
BF16 = jnp.bfloat16

EPS = 1e-6
SB_HEAD_DIM = 64
CONV_KERNEL = 31
HIST = CONV_KERNEL - 1
HIST_PAD = 32
MEM_HEADS = 4
LANES = 128
SUBLANES = 8
VMEM_LIMIT = 56 * 1024 * 1024

IN_TM, IN_TN = 512, 1024
SB_TQ = 512
SB_CHUNK = 256
CONV_TM = 512
CONV_ROWS = 32
CONV_LANES = 256
CONV_SEQS = 8
MEM_TQ = 512
MERGE_TM, MERGE_TN = 256, 512
OUT_TM = 256


def _nt_dot(a, b):
    return lax.dot_general(a, b, (((1,), (1,)), ((), ())), preferred_element_type=F32)


def _dot(a, b):
    return jnp.dot(a, b, preferred_element_type=F32)


def _sigmoid(x):
    return 1.0 / (1.0 + jnp.exp(-x))


def _strict_lower_ones(n):
    j = np.arange(n)[:, None]
    s = np.arange(n)[None, :]
    return jnp.asarray((j > s).astype(np.float32), dtype=BF16)


def _rms_matmul_kernel(x_ref, g_ref, w_ref, o_ref, h_ref):
    @pl.when(pl.program_id(1) == 0)
    def _():
        x = x_ref[...]
        r = lax.rsqrt(jnp.mean(x * x, axis=-1, keepdims=True) + EPS)
        h_ref[...] = (x * r * g_ref[...]).astype(BF16)

    o_ref[...] = _dot(h_ref[...], w_ref[...]).astype(o_ref.dtype)


def _rms_matmul(x, gain, w, col_off, n_cols, tm, tn, name):
    m, k = x.shape
    tm = min(tm, m)
    off = col_off // tn
    return pl.pallas_call(
        _rms_matmul_kernel,
        grid=(m // tm, n_cols // tn),
        in_specs=[
            pl.BlockSpec((tm, k), lambda i, j: (i, 0)),
            pl.BlockSpec((1, k), lambda i, j: (0, 0)),
            pl.BlockSpec((k, tn), lambda i, j: (0, j + off)),
        ],
        out_specs=pl.BlockSpec((tm, tn), lambda i, j: (i, j)),
        out_shape=jax.ShapeDtypeStruct((m, n_cols), F32),
        scratch_shapes=[pltpu.VMEM((tm, k), BF16)],
        compiler_params=pltpu.CompilerParams(
            dimension_semantics=("parallel", "arbitrary"), vmem_limit_bytes=VMEM_LIMIT),
        name=name,
    )(x, gain.reshape(1, k), w)


def _sb_unit(qm, kc, vc, u_tri, bias, carry, mask, feature_major=False):
    z = (_dot(qm, kc) if feature_major else _nt_dot(qm, kc)) + bias
    l = jnp.log(1.0 + jnp.exp(-jnp.abs(z)))
    log_fail = -(jnp.maximum(z, 0.0) + l)
    log_beta = jnp.minimum(z, 0.0) - l
    if mask is not None:
        log_fail = jnp.where(mask, log_fail, 0.0)
    hi = log_fail.astype(BF16)
    lo = (log_fail - hi.astype(F32)).astype(BF16)
    suffix = _dot(hi, u_tri) + _dot(lo, u_tri)
    w = jnp.exp(log_beta + suffix + carry)
    if mask is not None:
        w = jnp.where(mask, w, 0.0)
    wb = w.astype(BF16)
    o = _nt_dot(wb, vc) if feature_major else _dot(wb, vc)
    return o, carry + (suffix + log_fail)[:, 0:1]


def _sb_prompt_kernel(qi_ref, kj_ref, bias_ref, q_ref, k_ref, v_ref, u_ref, o_ref,
                      qm_ref, acc_ref, carry_ref):
    hp = pl.program_id(0)
    p = pl.program_id(1)
    qi = qi_ref[p]
    kj = kj_ref[p]
    tq = q_ref.shape[0]
    lane = lax.broadcasted_iota(jnp.int32, (tq, LANES), 1)
    first_head = lane < SB_HEAD_DIM

    @pl.when(kj == qi)
    def _():
        q = q_ref[...] * (SB_HEAD_DIM ** -0.5)
        qm_ref[0] = jnp.where(first_head, q, 0.0).astype(BF16)
        qm_ref[1] = jnp.where(first_head, 0.0, q).astype(BF16)
        acc_ref[...] = jnp.zeros_like(acc_ref)
        carry_ref[...] = jnp.zeros_like(carry_ref)

    def sweep(diagonal):
        n_chunks = k_ref.shape[0] // SB_CHUNK
        u_tri = u_ref[...]
        for c in reversed(range(n_chunks)):
            kc = k_ref[c * SB_CHUNK:(c + 1) * SB_CHUNK, :].astype(BF16)
            vc = v_ref[c * SB_CHUNK:(c + 1) * SB_CHUNK, :].astype(BF16)
            mask = None
            if diagonal:
                row = lax.broadcasted_iota(jnp.int32, (tq, SB_CHUNK), 0)
                col = lax.broadcasted_iota(jnp.int32, (tq, SB_CHUNK), 1)
                mask = (col + c * SB_CHUNK) < row
            outs = []
            for h in range(2):
                o, carry = _sb_unit(qm_ref[h], kc, vc, u_tri, bias_ref[2 * hp + h],
                                    carry_ref[h], mask)
                carry_ref[h] = carry
                outs.append(o)
            acc_ref[...] += jnp.where(first_head, outs[0], outs[1])

    @pl.when(kj == qi)
    def _():
        sweep(True)

    @pl.when(kj < qi)
    def _():
        sweep(False)

    @pl.when(kj == 0)
    def _():
        o_ref[...] = acc_ref[...]


def _sb_prompt(z, sb_bias, width, q_off, k_off, v_off):
    t = z.shape[0]
    tq = SB_TQ
    nq = t // tq
    qi = np.concatenate([np.full(i + 1, i) for i in range(nq)]).astype(np.int32)
    kj = np.concatenate([np.arange(i, -1, -1) for i in range(nq)]).astype(np.int32)
    n_hp = width // LANES
    grid_spec = pltpu.PrefetchScalarGridSpec(
        num_scalar_prefetch=3,
        grid=(n_hp, len(qi)),
        in_specs=[
            pl.BlockSpec((tq, LANES), lambda hp, p, qi, kj, b: (qi[p], q_off // LANES + hp)),
            pl.BlockSpec((tq, LANES), lambda hp, p, qi, kj, b: (kj[p], k_off // LANES + hp)),
            pl.BlockSpec((tq, LANES), lambda hp, p, qi, kj, b: (kj[p], v_off // LANES + hp)),
            pl.BlockSpec((SB_CHUNK, SB_CHUNK), lambda hp, p, qi, kj, b: (0, 0)),
        ],
        out_specs=pl.BlockSpec((tq, LANES), lambda hp, p, qi, kj, b: (qi[p], hp)),
        scratch_shapes=[
            pltpu.VMEM((2, tq, LANES), BF16),
            pltpu.VMEM((tq, LANES), F32),
            pltpu.VMEM((2, tq, 1), F32),
        ],
    )
    return pl.pallas_call(
        _sb_prompt_kernel,
        grid_spec=grid_spec,
        out_shape=jax.ShapeDtypeStruct((t, width), F32),
        compiler_params=pltpu.CompilerParams(
            dimension_semantics=("parallel", "arbitrary"), vmem_limit_bytes=VMEM_LIMIT),
        name="sb_prompt",
    )(jnp.asarray(qi), jnp.asarray(kj), sb_bias, z, z, z, _strict_lower_ones(SB_CHUNK))


def _sb_sample_kernel(pt_ref, q_ref, kn_ref, vn_ref, kp_ref, vp_ref, bias_ref, u_ref, o_ref,
                      qbd_ref, acc_ref, carry_ref, *, n_heads, n_pages):
    jj = pl.program_id(1)
    n_new, width = q_ref.shape
    rows = n_heads * n_new
    page = kp_ref.shape[2]
    u_tri = u_ref[...]
    bias = bias_ref[...]

    def update(kc, vc, mask, feature_major):
        o, carry = _sb_unit(qbd_ref[...], kc, vc, u_tri, bias, carry_ref[...], mask, feature_major)
        carry_ref[...] = carry
        acc_ref[...] += o

    @pl.when(jj == 0)
    def _():
        q = q_ref[...] * (SB_HEAD_DIM ** -0.5)
        q_rep = jnp.concatenate([q] * n_heads, axis=0)
        row = lax.broadcasted_iota(jnp.int32, (rows, width), 0)
        col = lax.broadcasted_iota(jnp.int32, (rows, width), 1)
        qbd_ref[...] = jnp.where(row // n_new == col // SB_HEAD_DIM, q_rep, 0.0).astype(BF16)
        acc_ref[...] = jnp.zeros_like(acc_ref)
        carry_ref[...] = jnp.zeros_like(carry_ref)
        pad = jnp.zeros((page - n_new, width), F32)
        kc = jnp.concatenate([kn_ref[...], pad], axis=0).astype(BF16)
        vc = jnp.concatenate([vn_ref[...], pad], axis=0).astype(BF16)
        r = lax.broadcasted_iota(jnp.int32, (rows, page), 0)
        s = lax.broadcasted_iota(jnp.int32, (rows, page), 1)
        update(kc, vc, s < r % n_new, False)

    update(kp_ref[0].astype(BF16), vp_ref[0].astype(BF16), None, True)

    @pl.when(jj == n_pages - 1)
    def _():
        col = lax.broadcasted_iota(jnp.int32, (n_new, width), 1)
        o = jnp.zeros((n_new, width), F32)
        for h in range(n_heads):
            o = o + jnp.where(col // SB_HEAD_DIM == h, acc_ref[h * n_new:(h + 1) * n_new, :], 0.0)
        o_ref[...] = o


def _sb_sample(z, cache_k, cache_v, page_table, sb_bias, n_new, width, q_off, k_off, v_off):
    n_seq, n_pages = page_table.shape
    n_phys, page = cache_k.shape[0], cache_k.shape[1]
    n_heads = width // SB_HEAD_DIM
    rows = n_heads * n_new
    assert page == LANES and n_new == SUBLANES
    kp = jnp.transpose(cache_k, (0, 2, 3, 1)).reshape(n_phys, width, page)
    vp = jnp.transpose(cache_v, (0, 2, 3, 1)).reshape(n_phys, width, page)
    bias_col = jnp.repeat(sb_bias, n_new).reshape(rows, 1)
    grid_spec = pltpu.PrefetchScalarGridSpec(
        num_scalar_prefetch=1,
        grid=(n_seq, n_pages),
        in_specs=[
            pl.BlockSpec((n_new, width), lambda b, jj, pt: (b, q_off // width)),
            pl.BlockSpec((n_new, width), lambda b, jj, pt: (b, k_off // width)),
            pl.BlockSpec((n_new, width), lambda b, jj, pt: (b, v_off // width)),
            pl.BlockSpec((1, width, page),
                         lambda b, jj, pt: (pt[b * n_pages + n_pages - 1 - jj], 0, 0)),
            pl.BlockSpec((1, width, page),
                         lambda b, jj, pt: (pt[b * n_pages + n_pages - 1 - jj], 0, 0)),
            pl.BlockSpec((rows, 1), lambda b, jj, pt: (0, 0)),
            pl.BlockSpec((page, page), lambda b, jj, pt: (0, 0)),
        ],
        out_specs=pl.BlockSpec((n_new, width), lambda b, jj, pt: (b, 0)),
        scratch_shapes=[
            pltpu.VMEM((rows, width), BF16),
            pltpu.VMEM((rows, width), F32),
            pltpu.VMEM((rows, 1), F32),
        ],
    )
    return pl.pallas_call(
        functools.partial(_sb_sample_kernel, n_heads=n_heads, n_pages=n_pages),
        grid_spec=grid_spec,
        out_shape=jax.ShapeDtypeStruct((n_seq * n_new, width), F32),
        compiler_params=pltpu.CompilerParams(
            dimension_semantics=("parallel", "arbitrary"), vmem_limit_bytes=VMEM_LIMIT),
        name="sb_sample",
    )(page_table.reshape(-1), z, z, z, kp, vp, bias_col, _strict_lower_ones(page))


def _ln_swish(c, g, b):
    mu = jnp.mean(c, axis=-1, keepdims=True)
    d = c - mu
    var = jnp.mean(d * d, axis=-1, keepdims=True)
    y = d * lax.rsqrt(var + EPS) * g + b
    return y * _sigmoid(y)


def _conv_prompt_kernel(a_ref, b_ref, w_ref, cb_ref, g_ref, lb_ref, o_ref, hist_ref,
                        u_ref, c_ref):
    tm, width = a_ref.shape

    @pl.when(pl.program_id(0) == 0)
    def _():
        u_ref[0:HIST_PAD, :] = jnp.zeros((HIST_PAD, width), F32)

    u_ref[HIST_PAD:HIST_PAD + tm, :] = a_ref[...] * _sigmoid(b_ref[...])

    def row_chunk(r, carry):
        base = pl.multiple_of(r * CONV_ROWS, CONV_ROWS)
        for lc in range(width // CONV_LANES):
            lanes = slice(lc * CONV_LANES, (lc + 1) * CONV_LANES)
            win = u_ref[pl.ds(base, CONV_ROWS + HIST_PAD), lanes]
            acc = jnp.broadcast_to(cb_ref[:, lanes], (CONV_ROWS, CONV_LANES))
            for k in range(CONV_KERNEL):
                s = HIST_PAD - HIST + k
                acc = acc + win[s:s + CONV_ROWS, :] * w_ref[k:k + 1, lanes]
            c_ref[pl.ds(base, CONV_ROWS), lanes] = acc
        return carry

    lax.fori_loop(0, tm // CONV_ROWS, row_chunk, 0)
    o_ref[...] = _ln_swish(c_ref[...], g_ref[...], lb_ref[...])
    tail = u_ref[tm:tm + HIST_PAD, :]
    hist_ref[...] = tail
    u_ref[0:HIST_PAD, :] = tail


def _conv_prompt(z, conv_w, conv_b, ln_g, ln_b, a_off, b_off):
    t = z.shape[0]
    kk, width = conv_w.shape
    tm = CONV_TM
    vec = lambda: pl.BlockSpec((1, width), lambda i: (0, 0))
    return pl.pallas_call(
        _conv_prompt_kernel,
        grid=(t // tm,),
        in_specs=[
            pl.BlockSpec((tm, width), lambda i: (i, a_off // width)),
            pl.BlockSpec((tm, width), lambda i: (i, b_off // width)),
            pl.BlockSpec((kk, width), lambda i: (0, 0)),
            vec(), vec(), vec(),
        ],
        out_specs=[
            pl.BlockSpec((tm, width), lambda i: (i, 0)),
            pl.BlockSpec((HIST_PAD, width), lambda i: (0, 0)),
        ],
        out_shape=[
            jax.ShapeDtypeStruct((t, width), F32),
            jax.ShapeDtypeStruct((HIST_PAD, width), F32),
        ],
        scratch_shapes=[
            pltpu.VMEM((tm + HIST_PAD, width), F32),
            pltpu.VMEM((tm, width), F32),
        ],
        compiler_params=pltpu.CompilerParams(
            dimension_semantics=("arbitrary",), vmem_limit_bytes=VMEM_LIMIT),
        name="conv_prompt",
    )(z, z, conv_w, conv_b.reshape(1, width), ln_g.reshape(1, width), ln_b.reshape(1, width))


def _conv_sample_kernel(a_ref, b_ref, hist_ref, w_ref, cb_ref, g_ref, lb_ref, o_ref, nh_ref,
                        u_ref, *, n_new):
    n_seqs = hist_ref.shape[0]
    width = a_ref.shape[1]

    def one_seq(s, carry):
        rows = pl.ds(pl.multiple_of(s * n_new, n_new), n_new)
        u_ref[0:HIST, :] = hist_ref[s]
        u_ref[HIST:HIST + n_new, :] = a_ref[rows, :] * _sigmoid(b_ref[rows, :])
        acc = jnp.broadcast_to(cb_ref[...], (n_new, width))
        for k in range(CONV_KERNEL):
            acc = acc + u_ref[k:k + n_new, :] * w_ref[k:k + 1, :]
        o_ref[rows, :] = _ln_swish(acc, g_ref[...], lb_ref[...])
        nh_ref[s] = u_ref[n_new:n_new + HIST, :]
        return carry

    lax.fori_loop(0, n_seqs, one_seq, 0)


def _conv_sample(z, cache_conv, conv_w, conv_b, ln_g, ln_b, n_new, a_off, b_off):
    n_seq = cache_conv.shape[0]
    kk, width = conv_w.shape
    ns = CONV_SEQS
    vec = lambda: pl.BlockSpec((1, width), lambda i: (0, 0))
    return pl.pallas_call(
        functools.partial(_conv_sample_kernel, n_new=n_new),
        grid=(n_seq // ns,),
        in_specs=[
            pl.BlockSpec((ns * n_new, width), lambda i: (i, a_off // width)),
            pl.BlockSpec((ns * n_new, width), lambda i: (i, b_off // width)),
            pl.BlockSpec((ns, HIST, width), lambda i: (i, 0, 0)),
            pl.BlockSpec((kk, width), lambda i: (0, 0)),
            vec(), vec(), vec(),
        ],
        out_specs=[
            pl.BlockSpec((ns * n_new, width), lambda i: (i, 0)),
            pl.BlockSpec((ns, HIST, width), lambda i: (i, 0, 0)),
        ],
        out_shape=[
            jax.ShapeDtypeStruct((n_seq * n_new, width), F32),
            jax.ShapeDtypeStruct((n_seq, HIST, width), F32),
        ],
        scratch_shapes=[pltpu.VMEM((HIST + n_new + 2, width), F32)],
        compiler_params=pltpu.CompilerParams(
            dimension_semantics=("parallel",), vmem_limit_bytes=VMEM_LIMIT),
        name="conv_sample",
    )(z, z, cache_conv, conv_w, conv_b.reshape(1, width), ln_g.reshape(1, width),
      ln_b.reshape(1, width))


def _mem_attn_kernel(q_ref, mk_ref, mv_ref, o_ref):
    width = q_ref.shape[1]
    hd = width // MEM_HEADS
    for h in range(MEM_HEADS):
        cols = slice(h * hd, (h + 1) * hd)
        q = (q_ref[:, cols] * (hd ** -0.5)).astype(BF16)
        s = _nt_dot(q, mk_ref[0, :, cols].astype(BF16))
        e = jnp.exp(s - jnp.max(s, axis=-1, keepdims=True))
        p = e / jnp.sum(e, axis=-1, keepdims=True)
        o_ref[:, cols] = _dot(p.astype(BF16), mv_ref[0, :, cols].astype(BF16))


def _mem_attn(z, mem_k, mem_v, tq, q_off):
    n_b, m, width = mem_k.shape
    nt = z.shape[0] // (n_b * tq)
    return pl.pallas_call(
        _mem_attn_kernel,
        grid=(n_b, nt),
        in_specs=[
            pl.BlockSpec((tq, width), lambda b, t: (b * nt + t, q_off // width)),
            pl.BlockSpec((1, m, width), lambda b, t: (b, 0, 0)),
            pl.BlockSpec((1, m, width), lambda b, t: (b, 0, 0)),
        ],
        out_specs=pl.BlockSpec((tq, width), lambda b, t: (b * nt + t, 0)),
        out_shape=jax.ShapeDtypeStruct((z.shape[0], width), F32),
        compiler_params=pltpu.CompilerParams(
            dimension_semantics=("parallel", "parallel"), vmem_limit_bytes=VMEM_LIMIT),
        name="mem_attn",
    )(z, mem_k, mem_v)


def _merge_kernel(osb_ref, ocv_ref, om_ref, gsb_ref, gcv_ref, gm_ref, msb_ref, mcv_ref, mm_ref,
                  mb_ref, wsb_ref, wcv_ref, wm_ref, o_ref, a_ref):
    @pl.when(pl.program_id(1) == 0)
    def _():
        for n, (o, g) in enumerate(((osb_ref, gsb_ref), (ocv_ref, gcv_ref), (om_ref, gm_ref))):
            gate = g[...]
            a_ref[n] = (o[...] * (gate * _sigmoid(gate))).astype(BF16)

    merged = _sigmoid(msb_ref[...] + mb_ref[0:1, :]) * _dot(a_ref[0], wsb_ref[...])
    merged += _sigmoid(mcv_ref[...] + mb_ref[1:2, :]) * _dot(a_ref[1], wcv_ref[...])
    merged += _sigmoid(mm_ref[...] + mb_ref[2:3, :]) * _dot(a_ref[2], wm_ref[...])
    o_ref[...] = merged.astype(o_ref.dtype)


def _merge(z, o_sb, o_cv, o_m, merge_bias, w_sb, w_cv, w_m, g_offs, m_offs):
    rows, width = o_sb.shape
    d = w_sb.shape[1]
    tm, tn = min(MERGE_TM, rows), MERGE_TN
    branch = lambda: pl.BlockSpec((tm, width), lambda i, j: (i, 0))
    gate = lambda off: pl.BlockSpec((tm, width), lambda i, j: (i, off // width))
    mix = lambda off: pl.BlockSpec((tm, tn), lambda i, j: (i, off // tn + j))
    weight = lambda: pl.BlockSpec((width, tn), lambda i, j: (0, j))
    return pl.pallas_call(
        _merge_kernel,
        grid=(rows // tm, d // tn),
        in_specs=[branch(), branch(), branch(),
                  gate(g_offs[0]), gate(g_offs[1]), gate(g_offs[2]),
                  mix(m_offs[0]), mix(m_offs[1]), mix(m_offs[2]),
                  pl.BlockSpec((3, tn), lambda i, j: (0, j)),
                  weight(), weight(), weight()],
        out_specs=pl.BlockSpec((tm, tn), lambda i, j: (i, j)),
        out_shape=jax.ShapeDtypeStruct((rows, d), BF16),
        scratch_shapes=[pltpu.VMEM((3, tm, width), BF16)],
        compiler_params=pltpu.CompilerParams(
            dimension_semantics=("parallel", "arbitrary"), vmem_limit_bytes=VMEM_LIMIT),
        name="merge",
    )(o_sb, o_cv, o_m, z, z, z, z, z, z, merge_bias, w_sb, w_cv, w_m)


def _out_kernel(x_ref, m_ref, w_ref, g_ref, o_ref, *, final_norm):
    y = x_ref[...] + _dot(m_ref[...], w_ref[...])
    if final_norm:
        r = lax.rsqrt(jnp.mean(y * y, axis=-1, keepdims=True) + EPS)
        y = y * r * g_ref[...]
    o_ref[...] = y


def _out_proj(x, merged, w_out, norm_final, final_norm):
    rows, d = x.shape
    tm = min(OUT_TM, rows)
    return pl.pallas_call(
        functools.partial(_out_kernel, final_norm=final_norm),
        grid=(rows // tm,),
        in_specs=[
            pl.BlockSpec((tm, d), lambda i: (i, 0)),
            pl.BlockSpec((tm, d), lambda i: (i, 0)),
            pl.BlockSpec((d, d), lambda i: (0, 0)),
            pl.BlockSpec((1, d), lambda i: (0, 0)),
        ],
        out_specs=pl.BlockSpec((tm, d), lambda i: (i, 0)),
        out_shape=jax.ShapeDtypeStruct((rows, d), F32),
        compiler_params=pltpu.CompilerParams(
            dimension_semantics=("parallel",), vmem_limit_bytes=VMEM_LIMIT),
        name="out_proj",
    )(x, merged, w_out, norm_final.reshape(1, d))


def kernel(x_prompt, x_sample, mem_prompt, cache_k, cache_v, cache_conv, cache_mem_k, cache_mem_v, page_table, norm_in, w_in, sb_bias, merge_bias, conv_w, conv_b, ln_g, ln_b, norm_mem, w_mem_kv, w_proj_sb, w_proj_cv, w_proj_mem, w_out, norm_final):
    bp, seq, d = x_prompt.shape
    bd, dec_seq, _ = x_sample.shape
    depth = w_in.shape[0]
    sb_w = cache_k.shape[3] * cache_k.shape[4]
    cv_w = conv_w.shape[2]
    mem_w = cache_mem_k.shape[3] * cache_mem_k.shape[4]
    mem_tokens = mem_prompt.shape[1]
    in_cols = w_in.shape[2]
    assert bp == 1 and in_cols == 4 * sb_w + 3 * cv_w + 2 * mem_w + 3 * d
    sizes = [sb_w] * 4 + [cv_w] * 3 + [mem_w] * 2 + [d] * 3
    (q_o, k_o, v_o, gsb_o, ga_o, gb_o, gcv_o, qm_o, gm_o, msb_o, mcv_o, mm_o) = (
        np.concatenate([[0], np.cumsum(sizes)[:-1]]).tolist())

    yp = x_prompt.reshape(seq, d)
    ys = x_sample.reshape(bd * dec_seq, d)
    outs = [[] for _ in range(8)]
    for l in range(depth):
        last = l == depth - 1
        w_in_l = w_in[l].astype(BF16)
        w_kv_l = w_mem_kv[l].astype(BF16)
        w_sb_l = w_proj_sb[l].astype(BF16)
        w_cv_l = w_proj_cv[l].astype(BF16)
        w_m_l = w_proj_mem[l].astype(BF16)
        w_out_l = w_out[l].astype(BF16)

        def tail(x, z, o_sb, o_cv, o_m):
            merged = _merge(z, o_sb, o_cv, o_m, merge_bias[l], w_sb_l, w_cv_l, w_m_l,
                            (gsb_o, gcv_o, gm_o), (msb_o, mcv_o, mm_o))
            return _out_proj(x, merged, w_out_l, norm_final, last)

        mem = mem_prompt.reshape(mem_tokens, d)
        mk = _rms_matmul(mem, norm_mem[l], w_kv_l, 0, mem_w, IN_TM, IN_TN, "mem_k")
        mv = _rms_matmul(mem, norm_mem[l], w_kv_l, mem_w, mem_w, IN_TM, IN_TN, "mem_v")
        zp = _rms_matmul(yp, norm_in[l], w_in_l, 0, in_cols, IN_TM, IN_TN, "in_proj_prompt")
        o_sb = _sb_prompt(zp, sb_bias[l], sb_w, q_o, k_o, v_o)
        o_cv, hist_p = _conv_prompt(zp, conv_w[l], conv_b[l], ln_g[l], ln_b[l], ga_o, gb_o)
        o_m = _mem_attn(zp, mk.reshape(1, mem_tokens, mem_w), mv.reshape(1, mem_tokens, mem_w),
                        MEM_TQ, qm_o)
        yp_new = tail(yp, zp, o_sb, o_cv, o_m)
        heads = cache_k.shape[3:]
        outs[0].append(zp[:, k_o:k_o + sb_w].reshape(bp, seq, *heads))
        outs[1].append(zp[:, v_o:v_o + sb_w].reshape(bp, seq, *heads))
        outs[2].append(hist_p[HIST_PAD - HIST:].reshape(bp, HIST, cv_w))
        outs[3].append(mk.reshape(bp, mem_tokens, *cache_mem_k.shape[3:]))
        outs[4].append(mv.reshape(bp, mem_tokens, *cache_mem_k.shape[3:]))
        yp = yp_new

        zs = _rms_matmul(ys, norm_in[l], w_in_l, 0, in_cols, IN_TM, IN_TN, "in_proj_sample")
        o_sb = _sb_sample(zs, cache_k[l], cache_v[l], page_table, sb_bias[l], dec_seq, sb_w,
                          q_o, k_o, v_o)
        o_cv, hist_s = _conv_sample(zs, cache_conv[l], conv_w[l], conv_b[l], ln_g[l], ln_b[l],
                                    dec_seq, ga_o, gb_o)
        o_m = _mem_attn(zs, cache_mem_k[l].reshape(bd, mem_tokens, mem_w),
                        cache_mem_v[l].reshape(bd, mem_tokens, mem_w), dec_seq, qm_o)
        ys_new = tail(ys, zs, o_sb, o_cv, o_m)
        outs[5].append(zs[:, k_o:k_o + sb_w].reshape(bd, dec_seq, *heads))
        outs[6].append(zs[:, v_o:v_o + sb_w].reshape(bd, dec_seq, *heads))
        outs[7].append(hist_s)
        ys = ys_new

    stacked = [jnp.stack(o) for o in outs]
    return (yp.reshape(bp, seq, d), ys.reshape(bd, dec_seq, d), *stacked)
```

```python
import functools

import jax
import jax.numpy as jnp
import numpy as np
from jax import lax
from jax.experimental import pallas as pl
from jax.experimental.pallas import tpu as pltpu

F32 = jnp.float32
BF16 = jnp.bfloat16

EPS = 1e-6
SB_HEAD_DIM = 64
CONV_KERNEL = 31
HIST = CONV_KERNEL - 1
HIST_PAD = 32
MEM_HEADS = 4
LANES = 128
SUBLANES = 8
VMEM_LIMIT = 56 * 1024 * 1024

IN_TM, IN_TN = 512, 1024
SB_TQ = 512
SB_CHUNK = 256
SB_GROUPS = 2
SB_PAGES = 8
CONV_TM = 512
CONV_ROWS = 32
CONV_LANES = 256
CONV_SEQS = 8
MEM_TQ = 512
MERGE_TM, MERGE_TN = 256, 512
OUT_TM = 256


def _nt_dot(a, b):
    return lax.dot_general(a, b, (((1,), (1,)), ((), ())), preferred_element_type=F32)


def _dot(a, b):
    return jnp.dot(a, b, preferred_element_type=F32)


def _sigmoid(x):
    return 1.0 / (1.0 + jnp.exp(-x))


def _strict_lower_ones(n):
    j = np.arange(n)[:, None]
    s = np.arange(n)[None, :]
    return jnp.asarray((j > s).astype(np.float32), dtype=BF16)


def _rms_matmul_kernel(x_ref, g_ref, w_ref, o_ref, h_ref):
    @pl.when(pl.program_id(1) == 0)
    def _():
        x = x_ref[...]
        r = lax.rsqrt(jnp.mean(x * x, axis=-1, keepdims=True) + EPS)
        h_ref[...] = (x * r * g_ref[...]).astype(BF16)

    o_ref[...] = _dot(h_ref[...], w_ref[...]).astype(o_ref.dtype)


def _rms_matmul(x, gain, w, col_off, n_cols, tm, tn, name):
    m, k = x.shape
    tm = min(tm, m)
    off = col_off // tn
    return pl.pallas_call(
        _rms_matmul_kernel,
        grid=(m // tm, n_cols // tn),
        in_specs=[
            pl.BlockSpec((tm, k), lambda i, j: (i, 0)),
            pl.BlockSpec((1, k), lambda i, j: (0, 0)),
            pl.BlockSpec((k, tn), lambda i, j: (0, j + off)),
        ],
        out_specs=pl.BlockSpec((tm, tn), lambda i, j: (i, j)),
        out_shape=jax.ShapeDtypeStruct((m, n_cols), F32),
        scratch_shapes=[pltpu.VMEM((tm, k), BF16)],
        compiler_params=pltpu.CompilerParams(
            dimension_semantics=("parallel", "arbitrary"), vmem_limit_bytes=VMEM_LIMIT),
        name=name,
    )(x, gain.reshape(1, k), w)


def _sb_logs(nz, mask):
    l = jnp.log(1.0 + jnp.exp(-jnp.abs(nz)))
    log_fail = jnp.minimum(nz, 0.0) - l
    log_beta = log_fail - nz
    if mask is not None:
        log_fail = jnp.where(mask, log_fail, 0.0)
    return log_fail, log_beta


def _sb_weights(log_fail, log_beta, suffix, carry, mask):
    w = jnp.exp(log_beta + suffix + carry)
    if mask is not None:
        w = jnp.where(mask, w, 0.0)
    return w.astype(BF16), carry + (suffix[:, 0:1] + log_fail[:, 0:1])


def _sb_prompt_kernel(qi_ref, kj_ref, bias_ref, q_ref, k_ref, v_ref, u_ref, o_ref,
                      qm_ref, vm_ref, acc_ref, carry_ref):
    p = pl.program_id(1)
    qi = qi_ref[p]
    kj = kj_ref[p]
    tq = q_ref.shape[0]
    tk = k_ref.shape[0]
    n_groups = q_ref.shape[1] // LANES
    n_chunks = tk // SB_CHUNK
    first_head = lax.broadcasted_iota(jnp.int32, (tq, LANES), 1) < SB_HEAD_DIM

    @pl.when(kj == qi)
    def _():
        for g in range(n_groups):
            q = q_ref[:, g * LANES:(g + 1) * LANES] * (-(SB_HEAD_DIM ** -0.5))
            qm_ref[2 * g] = jnp.where(first_head, q, 0.0).astype(BF16)
            qm_ref[2 * g + 1] = jnp.where(first_head, 0.0, q).astype(BF16)
        acc_ref[...] = jnp.zeros_like(acc_ref)
        carry_ref[...] = jnp.zeros_like(carry_ref)

    def sweep(diagonal):
        u_tri = u_ref[...]
        chunks = list(reversed(range(n_chunks)))
        masks = {c: None for c in chunks}
        if diagonal:
            row = lax.broadcasted_iota(jnp.int32, (tq, SB_CHUNK), 0)
            col = lax.broadcasted_iota(jnp.int32, (tq, SB_CHUNK), 1)
            masks = {c: (col + c * SB_CHUNK) < row for c in chunks}
        first_rows = lax.broadcasted_iota(jnp.int32, (tk, LANES), 1) < SB_HEAD_DIM
        for g in range(n_groups):
            v = v_ref[:, g * LANES:(g + 1) * LANES]
            vm_ref[g, 0:tk, :] = jnp.where(first_rows, v, 0.0).astype(BF16)
            vm_ref[g, tk:2 * tk, :] = jnp.where(first_rows, 0.0, v).astype(BF16)
        logs = {}
        for g in range(n_groups):
            for c in chunks:
                kc = k_ref[c * SB_CHUNK:(c + 1) * SB_CHUNK, g * LANES:(g + 1) * LANES].astype(BF16)
                for h in range(2):
                    head = (pl.program_id(0) * n_groups + g) * 2 + h
                    nz = _nt_dot(qm_ref[2 * g + h], kc) - bias_ref[head]
                    log_fail, log_beta = _sb_logs(nz, masks[c])
                    logs[g, c, h] = (log_fail.astype(BF16), log_fail[:, 0:1], log_beta)
        for g in range(n_groups):
            w = {}
            for h in range(2):
                carry = carry_ref[2 * g + h]
                for c in chunks:
                    lf16, lf_col, log_beta = logs[g, c, h]
                    w[h, c], carry = _sb_weights(lf_col, log_beta, _dot(lf16, u_tri), carry, masks[c])
                carry_ref[2 * g + h] = carry
            w_cat = jnp.concatenate([w[h, c] for h in range(2) for c in range(n_chunks)], axis=1)
            acc_ref[:, g * LANES:(g + 1) * LANES] += _dot(w_cat, vm_ref[g])

    @pl.when(kj == qi)
    def _():
        sweep(True)

    @pl.when(kj < qi)
    def _():
        sweep(False)

    @pl.when(kj == 0)
    def _():
        o_ref[...] = acc_ref[...]


def _sb_prompt(z, sb_bias, width, q_off, k_off, v_off):
    t = z.shape[0]
    tq = SB_TQ
    nq = t // tq
    qi = np.concatenate([np.full(i + 1, i) for i in range(nq)]).astype(np.int32)
    kj = np.concatenate([np.arange(i, -1, -1) for i in range(nq)]).astype(np.int32)
    bw = SB_GROUPS * LANES
    grid_spec = pltpu.PrefetchScalarGridSpec(
        num_scalar_prefetch=3,
        grid=(width // bw, len(qi)),
        in_specs=[
            pl.BlockSpec((tq, bw), lambda g, p, qi, kj, b: (qi[p], q_off // bw + g)),
            pl.BlockSpec((tq, bw), lambda g, p, qi, kj, b: (kj[p], k_off // bw + g)),
            pl.BlockSpec((tq, bw), lambda g, p, qi, kj, b: (kj[p], v_off // bw + g)),
            pl.BlockSpec((SB_CHUNK, SB_CHUNK), lambda g, p, qi, kj, b: (0, 0)),
        ],
        out_specs=pl.BlockSpec((tq, bw), lambda g, p, qi, kj, b: (qi[p], g)),
        scratch_shapes=[
            pltpu.VMEM((2 * SB_GROUPS, tq, LANES), BF16),
            pltpu.VMEM((SB_GROUPS, 2 * tq, LANES), BF16),
            pltpu.VMEM((tq, bw), F32),
            pltpu.VMEM((2 * SB_GROUPS, tq, 1), F32),
        ],
    )
    return pl.pallas_call(
        _sb_prompt_kernel,
        grid_spec=grid_spec,
        out_shape=jax.ShapeDtypeStruct((t, width), F32),
        compiler_params=pltpu.CompilerParams(
            dimension_semantics=("parallel", "arbitrary"), vmem_limit_bytes=VMEM_LIMIT),
        name="sb_prompt",
    )(jnp.asarray(qi), jnp.asarray(kj), sb_bias, z, z, z, _strict_lower_ones(SB_CHUNK))


def _sb_sample_kernel(pt_ref, q_ref, kn_ref, vn_ref, *rest, n_heads, n_pages, group):
    kp_refs, vp_refs = rest[:group], rest[group:2 * group]
    bias_ref, u_ref, o_ref, qbd_ref, acc_ref, carry_ref = rest[2 * group:]
    jj = pl.program_id(1)
    n_new, width = q_ref.shape
    rows = n_heads * n_new
    page = u_ref.shape[0]
    u_tri = u_ref[...]
    bias = bias_ref[...]

    @pl.when(jj == 0)
    def _():
        q = q_ref[...] * (-(SB_HEAD_DIM ** -0.5))
        q_rep = jnp.concatenate([q] * n_heads, axis=0)
        row = lax.broadcasted_iota(jnp.int32, (rows, width), 0)
        col = lax.broadcasted_iota(jnp.int32, (rows, width), 1)
        qbd_ref[...] = jnp.where(row // n_new == col // SB_HEAD_DIM, q_rep, 0.0).astype(BF16)
        pad = jnp.zeros((page - n_new, width), F32)
        kc = jnp.concatenate([kn_ref[...], pad], axis=0).astype(BF16)
        vc = jnp.concatenate([vn_ref[...], pad], axis=0).astype(BF16)
        r = lax.broadcasted_iota(jnp.int32, (rows, page), 0)
        s = lax.broadcasted_iota(jnp.int32, (rows, page), 1)
        mask = s < r % n_new
        log_fail, log_beta = _sb_logs(_nt_dot(qbd_ref[...], kc) - bias, mask)
        suffix = _dot(log_fail.astype(BF16), u_tri)
        w, carry = _sb_weights(log_fail, log_beta, suffix, jnp.zeros((rows, 1), F32), mask)
        carry_ref[...] = carry
        acc_ref[...] = _dot(w, vc)

    logs = [_sb_logs(_dot(qbd_ref[...], kp[0].astype(BF16)) - bias, None) for kp in kp_refs]
    suffix = [_dot(lf.astype(BF16), u_tri) for lf, _ in logs]
    carry = carry_ref[...]
    ws = []
    for (log_fail, log_beta), sfx in zip(logs, suffix):
        w, carry = _sb_weights(log_fail, log_beta, sfx, carry, None)
        ws.append(w)
    carry_ref[...] = carry
    vt = jnp.concatenate([vp[0].astype(BF16) for vp in vp_refs], axis=1)
    acc_ref[...] += _nt_dot(jnp.concatenate(ws, axis=1), vt)

    @pl.when(jj == n_pages // group - 1)
    def _():
        col = lax.broadcasted_iota(jnp.int32, (n_new, width), 1)
        o = jnp.zeros((n_new, width), F32)
        for h in range(n_heads):
            o = o + jnp.where(col // SB_HEAD_DIM == h, acc_ref[h * n_new:(h + 1) * n_new, :], 0.0)
        o_ref[...] = o


def _sb_sample(z, cache_k, cache_v, page_table, sb_bias, n_new, width, q_off, k_off, v_off):
    n_seq, n_pages = page_table.shape
    n_phys, page = cache_k.shape[0], cache_k.shape[1]
    n_heads = width // SB_HEAD_DIM
    rows = n_heads * n_new
    group = SB_PAGES
    assert page == LANES and n_new == SUBLANES and n_pages % group == 0
    kp = jnp.transpose(cache_k, (0, 2, 3, 1)).reshape(n_phys, width, page)
    vp = jnp.transpose(cache_v, (0, 2, 3, 1)).reshape(n_phys, width, page)
    bias_col = jnp.repeat(sb_bias, n_new).reshape(rows, 1)

    def page_spec(i):
        return pl.BlockSpec(
            (1, width, page),
            lambda b, jj, pt: (pt[b * n_pages + n_pages - 1 - (jj * group + i)], 0, 0))

    grid_spec = pltpu.PrefetchScalarGridSpec(
        num_scalar_prefetch=1,
        grid=(n_seq, n_pages // group),
        in_specs=[
            pl.BlockSpec((n_new, width), lambda b, jj, pt: (b, q_off // width)),
            pl.BlockSpec((n_new, width), lambda b, jj, pt: (b, k_off // width)),
            pl.BlockSpec((n_new, width), lambda b, jj, pt: (b, v_off // width)),
            *[page_spec(i) for i in range(group)],
            *[page_spec(i) for i in range(group)],
            pl.BlockSpec((rows, 1), lambda b, jj, pt: (0, 0)),
            pl.BlockSpec((page, page), lambda b, jj, pt: (0, 0)),
        ],
        out_specs=pl.BlockSpec((n_new, width), lambda b, jj, pt: (b, 0)),
        scratch_shapes=[
            pltpu.VMEM((rows, width), BF16),
            pltpu.VMEM((rows, width), F32),
            pltpu.VMEM((rows, 1), F32),
        ],
    )
    return pl.pallas_call(
        functools.partial(_sb_sample_kernel, n_heads=n_heads, n_pages=n_pages, group=group),
        grid_spec=grid_spec,
        out_shape=jax.ShapeDtypeStruct((n_seq * n_new, width), F32),
        compiler_params=pltpu.CompilerParams(
            dimension_semantics=("parallel", "arbitrary"), vmem_limit_bytes=VMEM_LIMIT),
        name="sb_sample",
    )(page_table.reshape(-1), z, z, z, *([kp] * group), *([vp] * group), bias_col,
      _strict_lower_ones(page))


def _ln_swish(c, g, b):
    mu = jnp.mean(c, axis=-1, keepdims=True)
    d = c - mu
    var = jnp.mean(d * d, axis=-1, keepdims=True)
    y = d * lax.rsqrt(var + EPS) * g + b
    return y * _sigmoid(y)


def _conv_prompt_kernel(a_ref, b_ref, w_ref, cb_ref, g_ref, lb_ref, o_ref, hist_ref,
                        u_ref, c_ref):
    tm, width = a_ref.shape

    @pl.when(pl.program_id(0) == 0)
    def _():
        u_ref[0:HIST_PAD, :] = jnp.zeros((HIST_PAD, width), F32)

    u_ref[HIST_PAD:HIST_PAD + tm, :] = a_ref[...] * _sigmoid(b_ref[...])

    def row_chunk(r, carry):
        base = pl.multiple_of(r * CONV_ROWS, CONV_ROWS)
        for lc in range(width // CONV_LANES):
            lanes = slice(lc * CONV_LANES, (lc + 1) * CONV_LANES)
            win = u_ref[pl.ds(base, CONV_ROWS + HIST_PAD), lanes]
            acc = jnp.broadcast_to(cb_ref[:, lanes], (CONV_ROWS, CONV_LANES))
            for k in range(CONV_KERNEL):
                s = HIST_PAD - HIST + k
                acc = acc + win[s:s + CONV_ROWS, :] * w_ref[k:k + 1, lanes]
            c_ref[pl.ds(base, CONV_ROWS), lanes] = acc
        return carry

    lax.fori_loop(0, tm // CONV_ROWS, row_chunk, 0)
    o_ref[...] = _ln_swish(c_ref[...], g_ref[...], lb_ref[...])
    tail = u_ref[tm:tm + HIST_PAD, :]
    hist_ref[...] = tail
    u_ref[0:HIST_PAD, :] = tail


def _conv_prompt(z, conv_w, conv_b, ln_g, ln_b, a_off, b_off):
    t = z.shape[0]
    kk, width = conv_w.shape
    tm = CONV_TM
    vec = lambda: pl.BlockSpec((1, width), lambda i: (0, 0))
    return pl.pallas_call(
        _conv_prompt_kernel,
        grid=(t // tm,),
        in_specs=[
            pl.BlockSpec((tm, width), lambda i: (i, a_off // width)),
            pl.BlockSpec((tm, width), lambda i: (i, b_off // width)),
            pl.BlockSpec((kk, width), lambda i: (0, 0)),
            vec(), vec(), vec(),
        ],
        out_specs=[
            pl.BlockSpec((tm, width), lambda i: (i, 0)),
            pl.BlockSpec((HIST_PAD, width), lambda i: (0, 0)),
        ],
        out_shape=[
            jax.ShapeDtypeStruct((t, width), F32),
            jax.ShapeDtypeStruct((HIST_PAD, width), F32),
        ],
        scratch_shapes=[
            pltpu.VMEM((tm + HIST_PAD, width), F32),
            pltpu.VMEM((tm, width), F32),
        ],
        compiler_params=pltpu.CompilerParams(
            dimension_semantics=("arbitrary",), vmem_limit_bytes=VMEM_LIMIT),
        name="conv_prompt",
    )(z, z, conv_w, conv_b.reshape(1, width), ln_g.reshape(1, width), ln_b.reshape(1, width))


def _conv_sample_kernel(a_ref, b_ref, hist_ref, w_ref, cb_ref, g_ref, lb_ref, o_ref, nh_ref,
                        u_ref, *, n_new):
    n_seqs = hist_ref.shape[0]
    width = a_ref.shape[1]

    def one_seq(s, carry):
        rows = pl.ds(pl.multiple_of(s * n_new, n_new), n_new)
        u_ref[0:HIST, :] = hist_ref[s]
        u_ref[HIST:HIST + n_new, :] = a_ref[rows, :] * _sigmoid(b_ref[rows, :])
        acc = jnp.broadcast_to(cb_ref[...], (n_new, width))
        for k in range(CONV_KERNEL):
            acc = acc + u_ref[k:k + n_new, :] * w_ref[k:k + 1, :]
        o_ref[rows, :] = _ln_swish(acc, g_ref[...], lb_ref[...])
        nh_ref[s] = u_ref[n_new:n_new + HIST, :]
        return carry

    lax.fori_loop(0, n_seqs, one_seq, 0)


def _conv_sample(z, cache_conv, conv_w, conv_b, ln_g, ln_b, n_new, a_off, b_off):
    n_seq = cache_conv.shape[0]
    kk, width = conv_w.shape
    ns = CONV_SEQS
    vec = lambda: pl.BlockSpec((1, width), lambda i: (0, 0))
    return pl.pallas_call(
        functools.partial(_conv_sample_kernel, n_new=n_new),
        grid=(n_seq // ns,),
        in_specs=[
            pl.BlockSpec((ns * n_new, width), lambda i: (i, a_off // width)),
            pl.BlockSpec((ns * n_new, width), lambda i: (i, b_off // width)),
            pl.BlockSpec((ns, HIST, width), lambda i: (i, 0, 0)),
            pl.BlockSpec((kk, width), lambda i: (0, 0)),
            vec(), vec(), vec(),
        ],
        out_specs=[
            pl.BlockSpec((ns * n_new, width), lambda i: (i, 0)),
            pl.BlockSpec((ns, HIST, width), lambda i: (i, 0, 0)),
        ],
        out_shape=[
            jax.ShapeDtypeStruct((n_seq * n_new, width), F32),
            jax.ShapeDtypeStruct((n_seq, HIST, width), F32),
        ],
        scratch_shapes=[pltpu.VMEM((HIST + n_new + 2, width), F32)],
        compiler_params=pltpu.CompilerParams(
            dimension_semantics=("parallel",), vmem_limit_bytes=VMEM_LIMIT),
        name="conv_sample",
    )(z, z, cache_conv, conv_w, conv_b.reshape(1, width), ln_g.reshape(1, width),
      ln_b.reshape(1, width))


def _mem_attn_kernel(q_ref, mk_ref, mv_ref, o_ref, *, head_axis):
    width = q_ref.shape[1]
    hd = width // MEM_HEADS
    lane_tiles = hd // LANES

    def head_rows(ref, h):
        m = ref.shape[1] // (lane_tiles * MEM_HEADS)
        tiles = [ref[0, pl.ds(t * MEM_HEADS + h, m, stride=lane_tiles * MEM_HEADS), :]
                 for t in range(lane_tiles)]
        return jnp.concatenate(tiles, axis=1)

    cols = [slice(h * hd, (h + 1) * hd) for h in range(MEM_HEADS)]
    scores = []
    for h in range(MEM_HEADS):
        mk = head_rows(mk_ref, h) if head_axis else mk_ref[0, :, cols[h]]
        q = (q_ref[:, cols[h]] * (hd ** -0.5)).astype(BF16)
        scores.append(_nt_dot(q, mk.astype(BF16)))
    probs = []
    for s in scores:
        e = jnp.exp(s - jnp.max(s, axis=-1, keepdims=True))
        probs.append((e / jnp.sum(e, axis=-1, keepdims=True)).astype(BF16))
    for h in range(MEM_HEADS):
        mv = head_rows(mv_ref, h) if head_axis else mv_ref[0, :, cols[h]]
        o_ref[:, cols[h]] = _dot(probs[h], mv.astype(BF16))


def _mem_attn(z, mem_k, mem_v, tq, q_off):
    head_axis = mem_k.ndim == 4
    n_b, m = mem_k.shape[:2]
    width = mem_k.shape[2] * mem_k.shape[3] if head_axis else mem_k.shape[2]
    nt = z.shape[0] // (n_b * tq)
    if head_axis:
        def rows_view(x):
            x = x.reshape(n_b, m, MEM_HEADS, width // MEM_HEADS // LANES, LANES)
            return jnp.transpose(x, (0, 1, 3, 2, 4)).reshape(n_b, m * width // LANES, LANES)
        mem_k, mem_v = rows_view(mem_k), rows_view(mem_v)
        mem_spec = pl.BlockSpec((1, m * width // LANES, LANES), lambda b, t: (b, 0, 0))
    else:
        mem_spec = pl.BlockSpec((1, m, width), lambda b, t: (b, 0, 0))
    return pl.pallas_call(
        functools.partial(_mem_attn_kernel, head_axis=head_axis),
        grid=(n_b, nt),
        in_specs=[
            pl.BlockSpec((tq, width), lambda b, t: (b * nt + t, q_off // width)),
            mem_spec, mem_spec,
        ],
        out_specs=pl.BlockSpec((tq, width), lambda b, t: (b * nt + t, 0)),
        out_shape=jax.ShapeDtypeStruct((z.shape[0], width), F32),
        compiler_params=pltpu.CompilerParams(
            dimension_semantics=("parallel", "parallel"), vmem_limit_bytes=VMEM_LIMIT),
        name="mem_attn",
    )(z, mem_k, mem_v)


def _merge_kernel(osb_ref, ocv_ref, om_ref, gsb_ref, gcv_ref, gm_ref, msb_ref, mcv_ref, mm_ref,
                  mb_ref, wsb_ref, wcv_ref, wm_ref, o_ref, a_ref):
    @pl.when(pl.program_id(1) == 0)
    def _():
        for n, (o, g) in enumerate(((osb_ref, gsb_ref), (ocv_ref, gcv_ref), (om_ref, gm_ref))):
            gate = g[...]
            a_ref[n] = (o[...] * (gate * _sigmoid(gate))).astype(BF16)

    merged = _sigmoid(msb_ref[...] + mb_ref[0:1, :]) * _dot(a_ref[0], wsb_ref[...])
    merged += _sigmoid(mcv_ref[...] + mb_ref[1:2, :]) * _dot(a_ref[1], wcv_ref[...])
    merged += _sigmoid(mm_ref[...] + mb_ref[2:3, :]) * _dot(a_ref[2], wm_ref[...])
    o_ref[...] = merged.astype(o_ref.dtype)


def _merge(z, o_sb, o_cv, o_m, merge_bias, w_sb, w_cv, w_m, g_offs, m_offs):
    rows, width = o_sb.shape
    d = w_sb.shape[1]
    tm, tn = min(MERGE_TM, rows), MERGE_TN
    branch = lambda: pl.BlockSpec((tm, width), lambda i, j: (i, 0))
    gate = lambda off: pl.BlockSpec((tm, width), lambda i, j: (i, off // width))
    mix = lambda off: pl.BlockSpec((tm, tn), lambda i, j: (i, off // tn + j))
    weight = lambda: pl.BlockSpec((width, tn), lambda i, j: (0, j))
    return pl.pallas_call(
        _merge_kernel,
        grid=(rows // tm, d // tn),
        in_specs=[branch(), branch(), branch(),
                  gate(g_offs[0]), gate(g_offs[1]), gate(g_offs[2]),
                  mix(m_offs[0]), mix(m_offs[1]), mix(m_offs[2]),
                  pl.BlockSpec((3, tn), lambda i, j: (0, j)),
                  weight(), weight(), weight()],
        out_specs=pl.BlockSpec((tm, tn), lambda i, j: (i, j)),
        out_shape=jax.ShapeDtypeStruct((rows, d), BF16),
        scratch_shapes=[pltpu.VMEM((3, tm, width), BF16)],
        compiler_params=pltpu.CompilerParams(
            dimension_semantics=("parallel", "arbitrary"), vmem_limit_bytes=VMEM_LIMIT),
        name="merge",
    )(o_sb, o_cv, o_m, z, z, z, z, z, z, merge_bias, w_sb, w_cv, w_m)


def _out_kernel(x_ref, m_ref, w_ref, g_ref, o_ref, *, final_norm):
    y = x_ref[...] + _dot(m_ref[...], w_ref[...])
    if final_norm:
        r = lax.rsqrt(jnp.mean(y * y, axis=-1, keepdims=True) + EPS)
        y = y * r * g_ref[...]
    o_ref[...] = y


def _out_proj(x, merged, w_out, norm_final, final_norm):
    rows, d = x.shape
    tm = min(OUT_TM, rows)
    return pl.pallas_call(
        functools.partial(_out_kernel, final_norm=final_norm),
        grid=(rows // tm,),
        in_specs=[
            pl.BlockSpec((tm, d), lambda i: (i, 0)),
            pl.BlockSpec((tm, d), lambda i: (i, 0)),
            pl.BlockSpec((d, d), lambda i: (0, 0)),
            pl.BlockSpec((1, d), lambda i: (0, 0)),
        ],
        out_specs=pl.BlockSpec((tm, d), lambda i: (i, 0)),
        out_shape=jax.ShapeDtypeStruct((rows, d), F32),
        compiler_params=pltpu.CompilerParams(
            dimension_semantics=("parallel",), vmem_limit_bytes=VMEM_LIMIT),
        name="out_proj",
    )(x, merged, w_out, norm_final.reshape(1, d))


def kernel(x_prompt, x_sample, mem_prompt, cache_k, cache_v, cache_conv, cache_mem_k, cache_mem_v, page_table, norm_in, w_in, sb_bias, merge_bias, conv_w, conv_b, ln_g, ln_b, norm_mem, w_mem_kv, w_proj_sb, w_proj_cv, w_proj_mem, w_out, norm_final):
    bp, seq, d = x_prompt.shape
    bd, dec_seq, _ = x_sample.shape
    depth = w_in.shape[0]
    sb_w = cache_k.shape[3] * cache_k.shape[4]
    cv_w = conv_w.shape[2]
    mem_w = cache_mem_k.shape[3] * cache_mem_k.shape[4]
    mem_tokens = mem_prompt.shape[1]
    in_cols = w_in.shape[2]
    assert bp == 1 and in_cols == 4 * sb_w + 3 * cv_w + 2 * mem_w + 3 * d
    sizes = [sb_w] * 4 + [cv_w] * 3 + [mem_w] * 2 + [d] * 3
    (q_o, k_o, v_o, gsb_o, ga_o, gb_o, gcv_o, qm_o, gm_o, msb_o, mcv_o, mm_o) = (
        np.concatenate([[0], np.cumsum(sizes)[:-1]]).tolist())

    yp = x_prompt.reshape(seq, d)
    ys = x_sample.reshape(bd * dec_seq, d)
    outs = [[] for _ in range(8)]
    for l in range(depth):
        last = l == depth - 1
        w_in_l = w_in[l].astype(BF16)
        w_kv_l = w_mem_kv[l].astype(BF16)
        w_sb_l = w_proj_sb[l].astype(BF16)
        w_cv_l = w_proj_cv[l].astype(BF16)
        w_m_l = w_proj_mem[l].astype(BF16)
        w_out_l = w_out[l].astype(BF16)

        def tail(x, z, o_sb, o_cv, o_m):
            merged = _merge(z, o_sb, o_cv, o_m, merge_bias[l], w_sb_l, w_cv_l, w_m_l,
                            (gsb_o, gcv_o, gm_o), (msb_o, mcv_o, mm_o))
            return _out_proj(x, merged, w_out_l, norm_final, last)

        mem = mem_prompt.reshape(mem_tokens, d)
        mk = _rms_matmul(mem, norm_mem[l], w_kv_l, 0, mem_w, IN_TM, IN_TN, "mem_k")
        mv = _rms_matmul(mem, norm_mem[l], w_kv_l, mem_w, mem_w, IN_TM, IN_TN, "mem_v")
        zp = _rms_matmul(yp, norm_in[l], w_in_l, 0, in_cols, IN_TM, IN_TN, "in_proj_prompt")
        o_sb = _sb_prompt(zp, sb_bias[l], sb_w, q_o, k_o, v_o)
        o_cv, hist_p = _conv_prompt(zp, conv_w[l], conv_b[l], ln_g[l], ln_b[l], ga_o, gb_o)
        o_m = _mem_attn(zp, mk.reshape(1, mem_tokens, mem_w), mv.reshape(1, mem_tokens, mem_w),
                        MEM_TQ, qm_o)
        yp_new = tail(yp, zp, o_sb, o_cv, o_m)
        heads = cache_k.shape[3:]
        outs[0].append(zp[:, k_o:k_o + sb_w].reshape(bp, seq, *heads))
        outs[1].append(zp[:, v_o:v_o + sb_w].reshape(bp, seq, *heads))
        outs[2].append(hist_p[HIST_PAD - HIST:].reshape(bp, HIST, cv_w))
        outs[3].append(mk.reshape(bp, mem_tokens, *cache_mem_k.shape[3:]))
        outs[4].append(mv.reshape(bp, mem_tokens, *cache_mem_k.shape[3:]))
        yp = yp_new

        zs = _rms_matmul(ys, norm_in[l], w_in_l, 0, in_cols, IN_TM, IN_TN, "in_proj_sample")
        o_sb = _sb_sample(zs, cache_k[l], cache_v[l], page_table, sb_bias[l], dec_seq, sb_w,
                          q_o, k_o, v_o)
        o_cv, hist_s = _conv_sample(zs, cache_conv[l], conv_w[l], conv_b[l], ln_g[l], ln_b[l],
                                    dec_seq, ga_o, gb_o)
        o_m = _mem_attn(zs, cache_mem_k[l], cache_mem_v[l], dec_seq, qm_o)
        ys_new = tail(ys, zs, o_sb, o_cv, o_m)
        outs[5].append(zs[:, k_o:k_o + sb_w].reshape(bd, dec_seq, *heads))
        outs[6].append(zs[:, v_o:v_o + sb_w].reshape(bd, dec_seq, *heads))
        outs[7].append(hist_s)
        ys = ys_new

    stacked = [jnp.stack(o) for o in outs]
    return (yp.reshape(bp, seq, d), ys.reshape(bd, dec_seq, d), *stacked)
```

```python
import functools

import jax
import jax.numpy as jnp
import numpy as np
from jax import lax
from jax.experimental import pallas as pl
from jax.experimental.pallas import tpu as pltpu

F32 = jnp.float32
BF16 = jnp.bfloat16

EPS = 1e-6
SB_HEAD_DIM = 64
CONV_KERNEL = 31
HIST = CONV_KERNEL - 1
HIST_PAD = 32
MEM_HEADS = 4
LANES = 128
SUBLANES = 8
VMEM_LIMIT = 56 * 1024 * 1024

IN_TM, IN_TN = 1024, 1536
SB_TQ = 512
SB_CHUNK = 256
SB_GROUPS = 4
SB_PAGES = 16
CONV_TM = 512
CONV_ROWS = 32
CONV_LANES = 256
CONV_SEQS = 8
MEM_TQ = 512
MERGE_TM, MERGE_TN = 256, 512
OUT_TM = 256


def _nt_dot(a, b):
    return lax.dot_general(a, b, (((1,), (1,)), ((), ())), preferred_element_type=F32)


def _dot(a, b):
    return jnp.dot(a, b, preferred_element_type=F32)


def _sigmoid(x):
    return 1.0 / (1.0 + jnp.exp(-x))


def _strict_lower_ones(n):
    j = np.arange(n)[:, None]
    s = np.arange(n)[None, :]
    return jnp.asarray((j > s).astype(np.float32), dtype=BF16)


def _rms_matmul_kernel(x_ref, g_ref, w_ref, o_ref, h_ref):
    @pl.when(pl.program_id(1) == 0)
    def _():
        x = x_ref[...]
        r = lax.rsqrt(jnp.mean(x * x, axis=-1, keepdims=True) + EPS)
        h_ref[...] = (x * r * g_ref[...]).astype(BF16)

    o_ref[...] = _dot(h_ref[...], w_ref[...]).astype(o_ref.dtype)


def _rms_matmul(x, gain, w, col_off, n_cols, tm, tn, name):
    m, k = x.shape
    tm = min(tm, m)
    off = col_off // tn
    return pl.pallas_call(
        _rms_matmul_kernel,
        grid=(m // tm, n_cols // tn),
        in_specs=[
            pl.BlockSpec((tm, k), lambda i, j: (i, 0)),
            pl.BlockSpec((1, k), lambda i, j: (0, 0)),
            pl.BlockSpec((k, tn), lambda i, j: (0, j + off)),
        ],
        out_specs=pl.BlockSpec((tm, tn), lambda i, j: (i, j)),
        out_shape=jax.ShapeDtypeStruct((m, n_cols), F32),
        scratch_shapes=[pltpu.VMEM((tm, k), BF16)],
        compiler_params=pltpu.CompilerParams(
            dimension_semantics=("parallel", "arbitrary"), vmem_limit_bytes=VMEM_LIMIT),
        name=name,
    )(x, gain.reshape(1, k), w)


def _sb_logs(nz, mask):
    l = jnp.log(1.0 + jnp.exp(-jnp.abs(nz)))
    log_fail = jnp.minimum(nz, 0.0) - l
    log_beta = log_fail - nz
    if mask is not None:
        log_fail = jnp.where(mask, log_fail, 0.0)
    return log_fail, log_beta


def _sb_weights(log_fail, log_beta, suffix, carry, mask):
    w = jnp.exp(log_beta + suffix + carry)
    if mask is not None:
        w = jnp.where(mask, w, 0.0)
    return w.astype(BF16), carry + (suffix[:, 0:1] + log_fail[:, 0:1])


def _sb_prompt_kernel(qi_ref, kj_ref, bias_ref, q_ref, k_ref, v_ref, u_ref, o_ref,
                      qm_ref, vm_ref, acc_ref, carry_ref):
    p = pl.program_id(1)
    qi = qi_ref[p]
    kj = kj_ref[p]
    tq = q_ref.shape[0]
    tk = k_ref.shape[0]
    n_groups = q_ref.shape[1] // LANES
    n_chunks = tk // SB_CHUNK
    first_head = lax.broadcasted_iota(jnp.int32, (tq, LANES), 1) < SB_HEAD_DIM

    @pl.when(kj == qi)
    def _():
        for g in range(n_groups):
            q = q_ref[:, g * LANES:(g + 1) * LANES] * (-(SB_HEAD_DIM ** -0.5))
            qm_ref[2 * g] = jnp.where(first_head, q, 0.0).astype(BF16)
            qm_ref[2 * g + 1] = jnp.where(first_head, 0.0, q).astype(BF16)
        acc_ref[...] = jnp.zeros_like(acc_ref)
        carry_ref[...] = jnp.zeros_like(carry_ref)

    def sweep(diagonal):
        u_tri = u_ref[...]
        chunks = list(reversed(range(n_chunks)))
        masks = {c: None for c in chunks}
        if diagonal:
            row = lax.broadcasted_iota(jnp.int32, (tq, SB_CHUNK), 0)
            col = lax.broadcasted_iota(jnp.int32, (tq, SB_CHUNK), 1)
            masks = {c: (col + c * SB_CHUNK) < row for c in chunks}
        first_rows = lax.broadcasted_iota(jnp.int32, (tk, LANES), 1) < SB_HEAD_DIM
        for g in range(n_groups):
            v = v_ref[:, g * LANES:(g + 1) * LANES]
            vm_ref[g, 0:tk, :] = jnp.where(first_rows, v, 0.0).astype(BF16)
            vm_ref[g, tk:2 * tk, :] = jnp.where(first_rows, 0.0, v).astype(BF16)
        logs = {}
        for g in range(n_groups):
            for c in chunks:
                kc = k_ref[c * SB_CHUNK:(c + 1) * SB_CHUNK, g * LANES:(g + 1) * LANES].astype(BF16)
                for h in range(2):
                    head = (pl.program_id(0) * n_groups + g) * 2 + h
                    nz = _nt_dot(qm_ref[2 * g + h], kc) - bias_ref[head]
                    log_fail, log_beta = _sb_logs(nz, masks[c])
                    logs[g, c, h] = (log_fail.astype(BF16), log_fail[:, 0:1], log_beta)
        for g in range(n_groups):
            w = {}
            for h in range(2):
                carry = carry_ref[2 * g + h]
                for c in chunks:
                    lf16, lf_col, log_beta = logs[g, c, h]
                    w[h, c], carry = _sb_weights(lf_col, log_beta, _dot(lf16, u_tri), carry, masks[c])
                carry_ref[2 * g + h] = carry
            w_cat = jnp.concatenate([w[h, c] for h in range(2) for c in range(n_chunks)], axis=1)
            acc_ref[:, g * LANES:(g + 1) * LANES] += _dot(w_cat, vm_ref[g])

    @pl.when(kj == qi)
    def _():
        sweep(True)

    @pl.when(kj < qi)
    def _():
        sweep(False)

    @pl.when(kj == 0)
    def _():
        o_ref[...] = acc_ref[...]


def _sb_prompt(z, sb_bias, width, q_off, k_off, v_off):
    t = z.shape[0]
    tq = SB_TQ
    nq = t // tq
    qi = np.concatenate([np.full(i + 1, i) for i in range(nq)]).astype(np.int32)
    kj = np.concatenate([np.arange(i, -1, -1) for i in range(nq)]).astype(np.int32)
    bw = SB_GROUPS * LANES
    grid_spec = pltpu.PrefetchScalarGridSpec(
        num_scalar_prefetch=3,
        grid=(width // bw, len(qi)),
        in_specs=[
            pl.BlockSpec((tq, bw), lambda g, p, qi, kj, b: (qi[p], q_off // bw + g)),
            pl.BlockSpec((tq, bw), lambda g, p, qi, kj, b: (kj[p], k_off // bw + g)),
            pl.BlockSpec((tq, bw), lambda g, p, qi, kj, b: (kj[p], v_off // bw + g)),
            pl.BlockSpec((SB_CHUNK, SB_CHUNK), lambda g, p, qi, kj, b: (0, 0)),
        ],
        out_specs=pl.BlockSpec((tq, bw), lambda g, p, qi, kj, b: (qi[p], g)),
        scratch_shapes=[
            pltpu.VMEM((2 * SB_GROUPS, tq, LANES), BF16),
            pltpu.VMEM((SB_GROUPS, 2 * tq, LANES), BF16),
            pltpu.VMEM((tq, bw), F32),
            pltpu.VMEM((2 * SB_GROUPS, tq, 1), F32),
        ],
    )
    return pl.pallas_call(
        _sb_prompt_kernel,
        grid_spec=grid_spec,
        out_shape=jax.ShapeDtypeStruct((t, width), F32),
        compiler_params=pltpu.CompilerParams(
            dimension_semantics=("parallel", "arbitrary"), vmem_limit_bytes=VMEM_LIMIT),
        name="sb_prompt",
    )(jnp.asarray(qi), jnp.asarray(kj), sb_bias, z, z, z, _strict_lower_ones(SB_CHUNK))


def _sb_sample_kernel(pt_ref, q_ref, kn_ref, vn_ref, *rest, n_heads, n_pages, group):
    kp_refs, vp_refs = rest[:group], rest[group:2 * group]
    bias_ref, u_ref, o_ref, qbd_ref, acc_ref, carry_ref = rest[2 * group:]
    jj = pl.program_id(1)
    n_new, width = q_ref.shape
    rows = n_heads * n_new
    page = u_ref.shape[0]
    u_tri = u_ref[...]
    bias = bias_ref[...]

    @pl.when(jj == 0)
    def _():
        q = q_ref[...] * (-(SB_HEAD_DIM ** -0.5))
        q_rep = jnp.concatenate([q] * n_heads, axis=0)
        row = lax.broadcasted_iota(jnp.int32, (rows, width), 0)
        col = lax.broadcasted_iota(jnp.int32, (rows, width), 1)
        qbd_ref[...] = jnp.where(row // n_new == col // SB_HEAD_DIM, q_rep, 0.0).astype(BF16)
        pad = jnp.zeros((page - n_new, width), F32)
        kc = jnp.concatenate([kn_ref[...], pad], axis=0).astype(BF16)
        vc = jnp.concatenate([vn_ref[...], pad], axis=0).astype(BF16)
        r = lax.broadcasted_iota(jnp.int32, (rows, page), 0)
        s = lax.broadcasted_iota(jnp.int32, (rows, page), 1)
        mask = s < r % n_new
        log_fail, log_beta = _sb_logs(_nt_dot(qbd_ref[...], kc) - bias, mask)
        suffix = _dot(log_fail.astype(BF16), u_tri)
        w, carry = _sb_weights(log_fail, log_beta, suffix, jnp.zeros((rows, 1), F32), mask)
        carry_ref[...] = carry
        acc_ref[...] = _dot(w, vc)

    logs = [_sb_logs(_dot(qbd_ref[...], kp[0].astype(BF16)) - bias, None) for kp in kp_refs]
    suffix = [_dot(lf.astype(BF16), u_tri) for lf, _ in logs]
    carry = carry_ref[...]
    ws = []
    for (log_fail, log_beta), sfx in zip(logs, suffix):
        w, carry = _sb_weights(log_fail, log_beta, sfx, carry, None)
        ws.append(w)
    carry_ref[...] = carry
    vt = jnp.concatenate([vp[0].astype(BF16) for vp in vp_refs], axis=1)
    acc_ref[...] += _nt_dot(jnp.concatenate(ws, axis=1), vt)

    @pl.when(jj == n_pages // group - 1)
    def _():
        col = lax.broadcasted_iota(jnp.int32, (n_new, width), 1)
        o = jnp.zeros((n_new, width), F32)
        for h in range(n_heads):
            o = o + jnp.where(col // SB_HEAD_DIM == h, acc_ref[h * n_new:(h + 1) * n_new, :], 0.0)
        o_ref[...] = o


def _sb_sample(z, cache_k, cache_v, page_table, sb_bias, n_new, width, q_off, k_off, v_off):
    n_seq, n_pages = page_table.shape
    n_phys, page = cache_k.shape[0], cache_k.shape[1]
    n_heads = width // SB_HEAD_DIM
    rows = n_heads * n_new
    group = SB_PAGES
    assert page == LANES and n_new == SUBLANES and n_pages % group == 0
    kp = jnp.transpose(cache_k, (0, 2, 3, 1)).reshape(n_phys, width, page)
    vp = jnp.transpose(cache_v, (0, 2, 3, 1)).reshape(n_phys, width, page)
    bias_col = jnp.repeat(sb_bias, n_new).reshape(rows, 1)

    def page_spec(i):
        return pl.BlockSpec(
            (1, width, page),
            lambda b, jj, pt: (pt[b * n_pages + n_pages - 1 - (jj * group + i)], 0, 0))

    grid_spec = pltpu.PrefetchScalarGridSpec(
        num_scalar_prefetch=1,
        grid=(n_seq, n_pages // group),
        in_specs=[
            pl.BlockSpec((n_new, width), lambda b, jj, pt: (b, q_off // width)),
            pl.BlockSpec((n_new, width), lambda b, jj, pt: (b, k_off // width)),
            pl.BlockSpec((n_new, width), lambda b, jj, pt: (b, v_off // width)),
            *[page_spec(i) for i in range(group)],
            *[page_spec(i) for i in range(group)],
            pl.BlockSpec((rows, 1), lambda b, jj, pt: (0, 0)),
            pl.BlockSpec((page, page), lambda b, jj, pt: (0, 0)),
        ],
        out_specs=pl.BlockSpec((n_new, width), lambda b, jj, pt: (b, 0)),
        scratch_shapes=[
            pltpu.VMEM((rows, width), BF16),
            pltpu.VMEM((rows, width), F32),
            pltpu.VMEM((rows, 1), F32),
        ],
    )
    return pl.pallas_call(
        functools.partial(_sb_sample_kernel, n_heads=n_heads, n_pages=n_pages, group=group),
        grid_spec=grid_spec,
        out_shape=jax.ShapeDtypeStruct((n_seq * n_new, width), F32),
        compiler_params=pltpu.CompilerParams(
            dimension_semantics=("parallel", "arbitrary"), vmem_limit_bytes=VMEM_LIMIT),
        name="sb_sample",
    )(page_table.reshape(-1), z, z, z, *([kp] * group), *([vp] * group), bias_col,
      _strict_lower_ones(page))


def _ln_swish(c, g, b):
    mu = jnp.mean(c, axis=-1, keepdims=True)
    d = c - mu
    var = jnp.mean(d * d, axis=-1, keepdims=True)
    y = d * lax.rsqrt(var + EPS) * g + b
    return y * _sigmoid(y)


def _conv_prompt_kernel(a_ref, b_ref, w_ref, cb_ref, g_ref, lb_ref, o_ref, hist_ref,
                        u_ref, c_ref):
    tm, width = a_ref.shape

    @pl.when(pl.program_id(0) == 0)
    def _():
        u_ref[0:HIST_PAD, :] = jnp.zeros((HIST_PAD, width), F32)

    u_ref[HIST_PAD:HIST_PAD + tm, :] = a_ref[...] * _sigmoid(b_ref[...])

    def row_chunk(r, carry):
        base = pl.multiple_of(r * CONV_ROWS, CONV_ROWS)
        for lc in range(width // CONV_LANES):
            lanes = slice(lc * CONV_LANES, (lc + 1) * CONV_LANES)
            win = u_ref[pl.ds(base, CONV_ROWS + HIST_PAD), lanes]
            acc = jnp.broadcast_to(cb_ref[:, lanes], (CONV_ROWS, CONV_LANES))
            for k in range(CONV_KERNEL):
                s = HIST_PAD - HIST + k
                acc = acc + win[s:s + CONV_ROWS, :] * w_ref[k:k + 1, lanes]
            c_ref[pl.ds(base, CONV_ROWS), lanes] = acc
        return carry

    lax.fori_loop(0, tm // CONV_ROWS, row_chunk, 0)
    o_ref[...] = _ln_swish(c_ref[...], g_ref[...], lb_ref[...])
    tail = u_ref[tm:tm + HIST_PAD, :]
    hist_ref[...] = tail
    u_ref[0:HIST_PAD, :] = tail


def _conv_prompt(z, conv_w, conv_b, ln_g, ln_b, a_off, b_off):
    t = z.shape[0]
    kk, width = conv_w.shape
    tm = CONV_TM
    vec = lambda: pl.BlockSpec((1, width), lambda i: (0, 0))
    return pl.pallas_call(
        _conv_prompt_kernel,
        grid=(t // tm,),
        in_specs=[
            pl.BlockSpec((tm, width), lambda i: (i, a_off // width)),
            pl.BlockSpec((tm, width), lambda i: (i, b_off // width)),
            pl.BlockSpec((kk, width), lambda i: (0, 0)),
            vec(), vec(), vec(),
        ],
        out_specs=[
            pl.BlockSpec((tm, width), lambda i: (i, 0)),
            pl.BlockSpec((HIST_PAD, width), lambda i: (0, 0)),
        ],
        out_shape=[
            jax.ShapeDtypeStruct((t, width), F32),
            jax.ShapeDtypeStruct((HIST_PAD, width), F32),
        ],
        scratch_shapes=[
            pltpu.VMEM((tm + HIST_PAD, width), F32),
            pltpu.VMEM((tm, width), F32),
        ],
        compiler_params=pltpu.CompilerParams(
            dimension_semantics=("arbitrary",), vmem_limit_bytes=VMEM_LIMIT),
        name="conv_prompt",
    )(z, z, conv_w, conv_b.reshape(1, width), ln_g.reshape(1, width), ln_b.reshape(1, width))


def _conv_sample_kernel(a_ref, b_ref, hist_ref, w_ref, cb_ref, g_ref, lb_ref, o_ref, nh_ref,
                        u_ref, *, n_new):
    n_seqs = hist_ref.shape[0]
    width = a_ref.shape[1]

    def one_seq(s, carry):
        rows = pl.ds(pl.multiple_of(s * n_new, n_new), n_new)
        u_ref[0:HIST, :] = hist_ref[s]
        u_ref[HIST:HIST + n_new, :] = a_ref[rows, :] * _sigmoid(b_ref[rows, :])
        acc = jnp.broadcast_to(cb_ref[...], (n_new, width))
        for k in range(CONV_KERNEL):
            acc = acc + u_ref[k:k + n_new, :] * w_ref[k:k + 1, :]
        o_ref[rows, :] = _ln_swish(acc, g_ref[...], lb_ref[...])
        nh_ref[s] = u_ref[n_new:n_new + HIST, :]
        return carry

    lax.fori_loop(0, n_seqs, one_seq, 0)


def _conv_sample(z, cache_conv, conv_w, conv_b, ln_g, ln_b, n_new, a_off, b_off):
    n_seq = cache_conv.shape[0]
    kk, width = conv_w.shape
    ns = CONV_SEQS
    vec = lambda: pl.BlockSpec((1, width), lambda i: (0, 0))
    return pl.pallas_call(
        functools.partial(_conv_sample_kernel, n_new=n_new),
        grid=(n_seq // ns,),
        in_specs=[
            pl.BlockSpec((ns * n_new, width), lambda i: (i, a_off // width)),
            pl.BlockSpec((ns * n_new, width), lambda i: (i, b_off // width)),
            pl.BlockSpec((ns, HIST, width), lambda i: (i, 0, 0)),
            pl.BlockSpec((kk, width), lambda i: (0, 0)),
            vec(), vec(), vec(),
        ],
        out_specs=[
            pl.BlockSpec((ns * n_new, width), lambda i: (i, 0)),
            pl.BlockSpec((ns, HIST, width), lambda i: (i, 0, 0)),
        ],
        out_shape=[
            jax.ShapeDtypeStruct((n_seq * n_new, width), F32),
            jax.ShapeDtypeStruct((n_seq, HIST, width), F32),
        ],
        scratch_shapes=[pltpu.VMEM((HIST + n_new + 2, width), F32)],
        compiler_params=pltpu.CompilerParams(
            dimension_semantics=("parallel",), vmem_limit_bytes=VMEM_LIMIT),
        name="conv_sample",
    )(z, z, cache_conv, conv_w, conv_b.reshape(1, width), ln_g.reshape(1, width),
      ln_b.reshape(1, width))


def _mem_attn_kernel(q_ref, mk_ref, mv_ref, o_ref, *, head_axis):
    width = q_ref.shape[1]
    hd = width // MEM_HEADS
    lane_tiles = hd // LANES

    def head_rows(ref, h):
        m = ref.shape[1] // (lane_tiles * MEM_HEADS)
        tiles = [ref[0, pl.ds(t * MEM_HEADS + h, m, stride=lane_tiles * MEM_HEADS), :]
                 for t in range(lane_tiles)]
        return jnp.concatenate(tiles, axis=1)

    cols = [slice(h * hd, (h + 1) * hd) for h in range(MEM_HEADS)]
    scores = []
    for h in range(MEM_HEADS):
        mk = head_rows(mk_ref, h) if head_axis else mk_ref[0, :, cols[h]]
        q = (q_ref[:, cols[h]] * (hd ** -0.5)).astype(BF16)
        scores.append(_nt_dot(q, mk.astype(BF16)))
    probs = []
    for s in scores:
        e = jnp.exp(s - jnp.max(s, axis=-1, keepdims=True))
        probs.append((e / jnp.sum(e, axis=-1, keepdims=True)).astype(BF16))
    for h in range(MEM_HEADS):
        mv = head_rows(mv_ref, h) if head_axis else mv_ref[0, :, cols[h]]
        o_ref[:, cols[h]] = _dot(probs[h], mv.astype(BF16))


def _mem_attn(z, mem_k, mem_v, tq, q_off):
    head_axis = mem_k.ndim == 4
    n_b, m = mem_k.shape[:2]
    width = mem_k.shape[2] * mem_k.shape[3] if head_axis else mem_k.shape[2]
    nt = z.shape[0] // (n_b * tq)
    if head_axis:
        def rows_view(x):
            x = x.reshape(n_b, m, MEM_HEADS, width // MEM_HEADS // LANES, LANES)
            return jnp.transpose(x, (0, 1, 3, 2, 4)).reshape(n_b, m * width // LANES, LANES)
        mem_k, mem_v = rows_view(mem_k), rows_view(mem_v)
        mem_spec = pl.BlockSpec((1, m * width // LANES, LANES), lambda b, t: (b, 0, 0))
    else:
        mem_spec = pl.BlockSpec((1, m, width), lambda b, t: (b, 0, 0))
    return pl.pallas_call(
        functools.partial(_mem_attn_kernel, head_axis=head_axis),
        grid=(n_b, nt),
        in_specs=[
            pl.BlockSpec((tq, width), lambda b, t: (b * nt + t, q_off // width)),
            mem_spec, mem_spec,
        ],
        out_specs=pl.BlockSpec((tq, width), lambda b, t: (b * nt + t, 0)),
        out_shape=jax.ShapeDtypeStruct((z.shape[0], width), F32),
        compiler_params=pltpu.CompilerParams(
            dimension_semantics=("parallel", "parallel"), vmem_limit_bytes=VMEM_LIMIT),
        name="mem_attn",
    )(z, mem_k, mem_v)


def _merge_kernel(osb_ref, ocv_ref, om_ref, gsb_ref, gcv_ref, gm_ref, msb_ref, mcv_ref, mm_ref,
                  mb_ref, wsb_ref, wcv_ref, wm_ref, o_ref, a_ref):
    @pl.when(pl.program_id(1) == 0)
    def _():
        for n, (o, g) in enumerate(((osb_ref, gsb_ref), (ocv_ref, gcv_ref), (om_ref, gm_ref))):
            gate = g[...]
            a_ref[n] = (o[...] * (gate * _sigmoid(gate))).astype(BF16)

    merged = _sigmoid(msb_ref[...] + mb_ref[0:1, :]) * _dot(a_ref[0], wsb_ref[...])
    merged += _sigmoid(mcv_ref[...] + mb_ref[1:2, :]) * _dot(a_ref[1], wcv_ref[...])
    merged += _sigmoid(mm_ref[...] + mb_ref[2:3, :]) * _dot(a_ref[2], wm_ref[...])
    o_ref[...] = merged.astype(o_ref.dtype)


def _merge(z, o_sb, o_cv, o_m, merge_bias, w_sb, w_cv, w_m, g_offs, m_offs):
    rows, width = o_sb.shape
    d = w_sb.shape[1]
    tm, tn = min(MERGE_TM, rows), MERGE_TN
    branch = lambda: pl.BlockSpec((tm, width), lambda i, j: (i, 0))
    gate = lambda off: pl.BlockSpec((tm, width), lambda i, j: (i, off // width))
    mix = lambda off: pl.BlockSpec((tm, tn), lambda i, j: (i, off // tn + j))
    weight = lambda: pl.BlockSpec((width, tn), lambda i, j: (0, j))
    return pl.pallas_call(
        _merge_kernel,
        grid=(rows // tm, d // tn),
        in_specs=[branch(), branch(), branch(),
                  gate(g_offs[0]), gate(g_offs[1]), gate(g_offs[2]),
                  mix(m_offs[0]), mix(m_offs[1]), mix(m_offs[2]),
                  pl.BlockSpec((3, tn), lambda i, j: (0, j)),
                  weight(), weight(), weight()],
        out_specs=pl.BlockSpec((tm, tn), lambda i, j: (i, j)),
        out_shape=jax.ShapeDtypeStruct((rows, d), BF16),
        scratch_shapes=[pltpu.VMEM((3, tm, width), BF16)],
        compiler_params=pltpu.CompilerParams(
            dimension_semantics=("parallel", "arbitrary"), vmem_limit_bytes=VMEM_LIMIT),
        name="merge",
    )(o_sb, o_cv, o_m, z, z, z, z, z, z, merge_bias, w_sb, w_cv, w_m)


def _out_kernel(x_ref, m_ref, w_ref, g_ref, o_ref, *, final_norm):
    y = x_ref[...] + _dot(m_ref[...], w_ref[...])
    if final_norm:
        r = lax.rsqrt(jnp.mean(y * y, axis=-1, keepdims=True) + EPS)
        y = y * r * g_ref[...]
    o_ref[...] = y


def _out_proj(x, merged, w_out, norm_final, final_norm):
    rows, d = x.shape
    tm = min(OUT_TM, rows)
    return pl.pallas_call(
        functools.partial(_out_kernel, final_norm=final_norm),
        grid=(rows // tm,),
        in_specs=[
            pl.BlockSpec((tm, d), lambda i: (i, 0)),
            pl.BlockSpec((tm, d), lambda i: (i, 0)),
            pl.BlockSpec((d, d), lambda i: (0, 0)),
            pl.BlockSpec((1, d), lambda i: (0, 0)),
        ],
        out_specs=pl.BlockSpec((tm, d), lambda i: (i, 0)),
        out_shape=jax.ShapeDtypeStruct((rows, d), F32),
        compiler_params=pltpu.CompilerParams(
            dimension_semantics=("parallel",), vmem_limit_bytes=VMEM_LIMIT),
        name="out_proj",
    )(x, merged, w_out, norm_final.reshape(1, d))


def kernel(x_prompt, x_sample, mem_prompt, cache_k, cache_v, cache_conv, cache_mem_k, cache_mem_v, page_table, norm_in, w_in, sb_bias, merge_bias, conv_w, conv_b, ln_g, ln_b, norm_mem, w_mem_kv, w_proj_sb, w_proj_cv, w_proj_mem, w_out, norm_final):
    bp, seq, d = x_prompt.shape
    bd, dec_seq, _ = x_sample.shape
    depth = w_in.shape[0]
    sb_w = cache_k.shape[3] * cache_k.shape[4]
    cv_w = conv_w.shape[2]
    mem_w = cache_mem_k.shape[3] * cache_mem_k.shape[4]
    mem_tokens = mem_prompt.shape[1]
    in_cols = w_in.shape[2]
    assert bp == 1 and in_cols == 4 * sb_w + 3 * cv_w + 2 * mem_w + 3 * d
    sizes = [sb_w] * 4 + [cv_w] * 3 + [mem_w] * 2 + [d] * 3
    (q_o, k_o, v_o, gsb_o, ga_o, gb_o, gcv_o, qm_o, gm_o, msb_o, mcv_o, mm_o) = (
        np.concatenate([[0], np.cumsum(sizes)[:-1]]).tolist())

    yp = x_prompt.reshape(seq, d)
    ys = x_sample.reshape(bd * dec_seq, d)
    outs = [[] for _ in range(8)]
    for l in range(depth):
        last = l == depth - 1
        w_in_l = w_in[l].astype(BF16)
        w_kv_l = w_mem_kv[l].astype(BF16)
        w_sb_l = w_proj_sb[l].astype(BF16)
        w_cv_l = w_proj_cv[l].astype(BF16)
        w_m_l = w_proj_mem[l].astype(BF16)
        w_out_l = w_out[l].astype(BF16)

        def tail(x, z, o_sb, o_cv, o_m):
            merged = _merge(z, o_sb, o_cv, o_m, merge_bias[l], w_sb_l, w_cv_l, w_m_l,
                            (gsb_o, gcv_o, gm_o), (msb_o, mcv_o, mm_o))
            return _out_proj(x, merged, w_out_l, norm_final, last)

        mem = mem_prompt.reshape(mem_tokens, d)
        mk = _rms_matmul(mem, norm_mem[l], w_kv_l, 0, mem_w, IN_TM, mem_w, "mem_k")
        mv = _rms_matmul(mem, norm_mem[l], w_kv_l, mem_w, mem_w, IN_TM, mem_w, "mem_v")
        zp = _rms_matmul(yp, norm_in[l], w_in_l, 0, in_cols, IN_TM, IN_TN, "in_proj_prompt")
        o_sb = _sb_prompt(zp, sb_bias[l], sb_w, q_o, k_o, v_o)
        o_cv, hist_p = _conv_prompt(zp, conv_w[l], conv_b[l], ln_g[l], ln_b[l], ga_o, gb_o)
        o_m = _mem_attn(zp, mk.reshape(1, mem_tokens, mem_w), mv.reshape(1, mem_tokens, mem_w),
                        MEM_TQ, qm_o)
        yp_new = tail(yp, zp, o_sb, o_cv, o_m)
        heads = cache_k.shape[3:]
        outs[0].append(zp[:, k_o:k_o + sb_w].reshape(bp, seq, *heads))
        outs[1].append(zp[:, v_o:v_o + sb_w].reshape(bp, seq, *heads))
        outs[2].append(hist_p[HIST_PAD - HIST:].reshape(bp, HIST, cv_w))
        outs[3].append(mk.reshape(bp, mem_tokens, *cache_mem_k.shape[3:]))
        outs[4].append(mv.reshape(bp, mem_tokens, *cache_mem_k.shape[3:]))
        yp = yp_new

        zs = _rms_matmul(ys, norm_in[l], w_in_l, 0, in_cols, IN_TM, IN_TN, "in_proj_sample")
        o_sb = _sb_sample(zs, cache_k[l], cache_v[l], page_table, sb_bias[l], dec_seq, sb_w,
                          q_o, k_o, v_o)
        o_cv, hist_s = _conv_sample(zs, cache_conv[l], conv_w[l], conv_b[l], ln_g[l], ln_b[l],
                                    dec_seq, ga_o, gb_o)
        o_m = _mem_attn(zs, cache_mem_k[l], cache_mem_v[l], dec_seq, qm_o)
        ys_new = tail(ys, zs, o_sb, o_cv, o_m)
        outs[5].append(zs[:, k_o:k_o + sb_w].reshape(bd, dec_seq, *heads))
        outs[6].append(zs[:, v_o:v_o + sb_w].reshape(bd, dec_seq, *heads))
        outs[7].append(hist_s)
        ys = ys_new

    stacked = [jnp.stack(o) for o in outs]
    return (yp.reshape(bp, seq, d), ys.reshape(bd, dec_seq, d), *stacked)
```

```python
import functools

import jax
import jax.numpy as jnp
import numpy as np
from jax import lax
from jax.experimental import pallas as pl
from jax.experimental.pallas import tpu as pltpu

F32 = jnp.float32
BF16 = jnp.bfloat16

EPS = 1e-6
SB_HEAD_DIM = 64
CONV_KERNEL = 31
HIST = CONV_KERNEL - 1
HIST_PAD = 32
MEM_HEADS = 4
LANES = 128
SUBLANES = 8
VMEM_LIMIT = 56 * 1024 * 1024

IN_TM, IN_TN = 1024, 1536
SB_TQ = 512
SB_CHUNK = 256
SB_GROUPS = 4
SB_PAGES = 16
CONV_TM = 512
CONV_ROWS = 32
CONV_LANES = 256
CONV_SEQS = 8
MEM_TQ = 512
MERGE_TM, MERGE_TN = 512, 512
OUT_TM = 256


def _nt_dot(a, b):
    return lax.dot_general(a, b, (((1,), (1,)), ((), ())), preferred_element_type=F32)


def _dot(a, b):
    return jnp.dot(a, b, preferred_element_type=F32)


def _sigmoid(x):
    return 1.0 / (1.0 + jnp.exp(-x))


def _strict_lower_ones(n):
    j = np.arange(n)[:, None]
    s = np.arange(n)[None, :]
    return jnp.asarray((j > s).astype(np.float32), dtype=BF16)


def _rms_matmul_kernel(x_ref, g_ref, w_ref, o_ref, h_ref):
    @pl.when(pl.program_id(1) == 0)
    def _():
        x = x_ref[...]
        r = lax.rsqrt(jnp.mean(x * x, axis=-1, keepdims=True) + EPS)
        h_ref[...] = (x * r * g_ref[...]).astype(BF16)

    o_ref[...] = _dot(h_ref[...], w_ref[...]).astype(o_ref.dtype)


def _rms_matmul(x, gain, w, col_off, n_cols, tm, tn, name, out_dtype=F32):
    m, k = x.shape
    tm = min(tm, m)
    off = col_off // tn
    return pl.pallas_call(
        _rms_matmul_kernel,
        grid=(m // tm, n_cols // tn),
        in_specs=[
            pl.BlockSpec((tm, k), lambda i, j: (i, 0)),
            pl.BlockSpec((1, k), lambda i, j: (0, 0)),
            pl.BlockSpec((k, tn), lambda i, j: (0, j + off)),
        ],
        out_specs=pl.BlockSpec((tm, tn), lambda i, j: (i, j)),
        out_shape=jax.ShapeDtypeStruct((m, n_cols), out_dtype),
        scratch_shapes=[pltpu.VMEM((tm, k), BF16)],
        compiler_params=pltpu.CompilerParams(
            dimension_semantics=("parallel", "arbitrary"), vmem_limit_bytes=VMEM_LIMIT),
        name=name,
    )(x, gain.reshape(1, k), w)


def _sb_logs(nz, mask):
    l = jnp.log(1.0 + jnp.exp(-jnp.abs(nz)))
    log_fail = jnp.minimum(nz, 0.0) - l
    log_beta = log_fail - nz
    if mask is not None:
        log_fail = jnp.where(mask, log_fail, 0.0)
    return log_fail, log_beta


def _sb_weights(log_fail, log_beta, suffix, carry, mask):
    w = jnp.exp(log_beta + suffix + carry)
    if mask is not None:
        w = jnp.where(mask, w, 0.0)
    return w.astype(BF16), carry + (suffix[:, 0:1] + log_fail[:, 0:1])


def _sb_prompt_kernel(qi_ref, kj_ref, bias_ref, q_ref, k_ref, v_ref, u_ref, o_ref,
                      qm_ref, vm_ref, acc_ref, carry_ref):
    p = pl.program_id(1)
    qi = qi_ref[p]
    kj = kj_ref[p]
    tq = q_ref.shape[0]
    tk = k_ref.shape[0]
    n_groups = q_ref.shape[1] // LANES
    n_chunks = tk // SB_CHUNK
    first_head = lax.broadcasted_iota(jnp.int32, (tq, LANES), 1) < SB_HEAD_DIM

    @pl.when(kj == qi)
    def _():
        for g in range(n_groups):
            q = q_ref[:, g * LANES:(g + 1) * LANES] * (-(SB_HEAD_DIM ** -0.5))
            qm_ref[2 * g] = jnp.where(first_head, q, 0.0).astype(BF16)
            qm_ref[2 * g + 1] = jnp.where(first_head, 0.0, q).astype(BF16)
        acc_ref[...] = jnp.zeros_like(acc_ref)
        carry_ref[...] = jnp.zeros_like(carry_ref)

    def sweep(diagonal):
        u_tri = u_ref[...]
        chunks = list(reversed(range(n_chunks)))
        masks = {c: None for c in chunks}
        if diagonal:
            row = lax.broadcasted_iota(jnp.int32, (tq, SB_CHUNK), 0)
            col = lax.broadcasted_iota(jnp.int32, (tq, SB_CHUNK), 1)
            masks = {c: (col + c * SB_CHUNK) < row for c in chunks}
        first_rows = lax.broadcasted_iota(jnp.int32, (tk, LANES), 1) < SB_HEAD_DIM
        for g in range(n_groups):
            v = v_ref[:, g * LANES:(g + 1) * LANES]
            vm_ref[g, 0:tk, :] = jnp.where(first_rows, v, 0.0).astype(BF16)
            vm_ref[g, tk:2 * tk, :] = jnp.where(first_rows, 0.0, v).astype(BF16)
        logs = {}
        for g in range(n_groups):
            for c in chunks:
                kc = k_ref[c * SB_CHUNK:(c + 1) * SB_CHUNK, g * LANES:(g + 1) * LANES].astype(BF16)
                for h in range(2):
                    head = (pl.program_id(0) * n_groups + g) * 2 + h
                    nz = _nt_dot(qm_ref[2 * g + h], kc) - bias_ref[head]
                    log_fail, log_beta = _sb_logs(nz, masks[c])
                    logs[g, c, h] = (log_fail.astype(BF16), log_fail[:, 0:1], log_beta)
        for g in range(n_groups):
            w = {}
            for h in range(2):
                carry = carry_ref[2 * g + h]
                for c in chunks:
                    lf16, lf_col, log_beta = logs[g, c, h]
                    w[h, c], carry = _sb_weights(lf_col, log_beta, _dot(lf16, u_tri), carry, masks[c])
                carry_ref[2 * g + h] = carry
            w_cat = jnp.concatenate([w[h, c] for h in range(2) for c in range(n_chunks)], axis=1)
            acc_ref[:, g * LANES:(g + 1) * LANES] += _dot(w_cat, vm_ref[g])

    @pl.when(kj == qi)
    def _():
        sweep(True)

    @pl.when(kj < qi)
    def _():
        sweep(False)

    @pl.when(kj == 0)
    def _():
        o_ref[...] = acc_ref[...]


def _sb_prompt(z, sb_bias, width, q_off, k_off, v_off):
    t = z.shape[0]
    tq = SB_TQ
    nq = t // tq
    qi = np.concatenate([np.full(i + 1, i) for i in range(nq)]).astype(np.int32)
    kj = np.concatenate([np.arange(i, -1, -1) for i in range(nq)]).astype(np.int32)
    bw = SB_GROUPS * LANES
    grid_spec = pltpu.PrefetchScalarGridSpec(
        num_scalar_prefetch=3,
        grid=(width // bw, len(qi)),
        in_specs=[
            pl.BlockSpec((tq, bw), lambda g, p, qi, kj, b: (qi[p], q_off // bw + g)),
            pl.BlockSpec((tq, bw), lambda g, p, qi, kj, b: (kj[p], k_off // bw + g)),
            pl.BlockSpec((tq, bw), lambda g, p, qi, kj, b: (kj[p], v_off // bw + g)),
            pl.BlockSpec((SB_CHUNK, SB_CHUNK), lambda g, p, qi, kj, b: (0, 0)),
        ],
        out_specs=pl.BlockSpec((tq, bw), lambda g, p, qi, kj, b: (qi[p], g)),
        scratch_shapes=[
            pltpu.VMEM((2 * SB_GROUPS, tq, LANES), BF16),
            pltpu.VMEM((SB_GROUPS, 2 * tq, LANES), BF16),
            pltpu.VMEM((tq, bw), F32),
            pltpu.VMEM((2 * SB_GROUPS, tq, 1), F32),
        ],
    )
    return pl.pallas_call(
        _sb_prompt_kernel,
        grid_spec=grid_spec,
        out_shape=jax.ShapeDtypeStruct((t, width), F32),
        compiler_params=pltpu.CompilerParams(
            dimension_semantics=("parallel", "arbitrary"), vmem_limit_bytes=VMEM_LIMIT),
        name="sb_prompt",
    )(jnp.asarray(qi), jnp.asarray(kj), sb_bias, z, z, z, _strict_lower_ones(SB_CHUNK))


def _sb_sample_kernel(pt_ref, q_ref, kn_ref, vn_ref, *rest, n_heads, n_pages, group):
    kp_refs, vp_refs = rest[:group], rest[group:2 * group]
    bias_ref, u_ref, o_ref, qbd_ref, acc_ref, carry_ref = rest[2 * group:]
    jj = pl.program_id(1)
    n_new, width = q_ref.shape
    rows = n_heads * n_new
    page = u_ref.shape[0]
    u_tri = u_ref[...]
    bias = bias_ref[...]

    @pl.when(jj == 0)
    def _():
        q = q_ref[...] * (-(SB_HEAD_DIM ** -0.5))
        q_rep = jnp.concatenate([q] * n_heads, axis=0)
        row = lax.broadcasted_iota(jnp.int32, (rows, width), 0)
        col = lax.broadcasted_iota(jnp.int32, (rows, width), 1)
        qbd_ref[...] = jnp.where(row // n_new == col // SB_HEAD_DIM, q_rep, 0.0).astype(BF16)
        pad = jnp.zeros((page - n_new, width), F32)
        kc = jnp.concatenate([kn_ref[...], pad], axis=0).astype(BF16)
        vc = jnp.concatenate([vn_ref[...], pad], axis=0).astype(BF16)
        r = lax.broadcasted_iota(jnp.int32, (rows, page), 0)
        s = lax.broadcasted_iota(jnp.int32, (rows, page), 1)
        mask = s < r % n_new
        log_fail, log_beta = _sb_logs(_nt_dot(qbd_ref[...], kc) - bias, mask)
        suffix = _dot(log_fail.astype(BF16), u_tri)
        w, carry = _sb_weights(log_fail, log_beta, suffix, jnp.zeros((rows, 1), F32), mask)
        carry_ref[...] = carry
        acc_ref[...] = _dot(w, vc)

    logs = [_sb_logs(_dot(qbd_ref[...], kp[0].astype(BF16)) - bias, None) for kp in kp_refs]
    suffix = [_dot(lf.astype(BF16), u_tri) for lf, _ in logs]
    carry = carry_ref[...]
    ws = []
    for (log_fail, log_beta), sfx in zip(logs, suffix):
        w, carry = _sb_weights(log_fail, log_beta, sfx, carry, None)
        ws.append(w)
    carry_ref[...] = carry
    vt = jnp.concatenate([vp[0].astype(BF16) for vp in vp_refs], axis=1)
    acc_ref[...] += _nt_dot(jnp.concatenate(ws, axis=1), vt)

    @pl.when(jj == n_pages // group - 1)
    def _():
        col = lax.broadcasted_iota(jnp.int32, (n_new, width), 1)
        o = jnp.zeros((n_new, width), F32)
        for h in range(n_heads):
            o = o + jnp.where(col // SB_HEAD_DIM == h, acc_ref[h * n_new:(h + 1) * n_new, :], 0.0)
        o_ref[...] = o


def _sb_sample(z, cache_k, cache_v, page_table, sb_bias, n_new, width, q_off, k_off, v_off):
    n_seq, n_pages = page_table.shape
    n_phys, page = cache_k.shape[0], cache_k.shape[1]
    n_heads = width // SB_HEAD_DIM
    rows = n_heads * n_new
    group = SB_PAGES
    assert page == LANES and n_new == SUBLANES and n_pages % group == 0
    kp = jnp.transpose(cache_k, (0, 2, 3, 1)).reshape(n_phys, width, page)
    vp = jnp.transpose(cache_v, (0, 2, 3, 1)).reshape(n_phys, width, page)
    bias_col = jnp.repeat(sb_bias, n_new).reshape(rows, 1)

    def page_spec(i):
        return pl.BlockSpec(
            (1, width, page),
            lambda b, jj, pt: (pt[b * n_pages + n_pages - 1 - (jj * group + i)], 0, 0))

    grid_spec = pltpu.PrefetchScalarGridSpec(
        num_scalar_prefetch=1,
        grid=(n_seq, n_pages // group),
        in_specs=[
            pl.BlockSpec((n_new, width), lambda b, jj, pt: (b, q_off // width)),
            pl.BlockSpec((n_new, width), lambda b, jj, pt: (b, k_off // width)),
            pl.BlockSpec((n_new, width), lambda b, jj, pt: (b, v_off // width)),
            *[page_spec(i) for i in range(group)],
            *[page_spec(i) for i in range(group)],
            pl.BlockSpec((rows, 1), lambda b, jj, pt: (0, 0)),
            pl.BlockSpec((page, page), lambda b, jj, pt: (0, 0)),
        ],
        out_specs=pl.BlockSpec((n_new, width), lambda b, jj, pt: (b, 0)),
        scratch_shapes=[
            pltpu.VMEM((rows, width), BF16),
            pltpu.VMEM((rows, width), F32),
            pltpu.VMEM((rows, 1), F32),
        ],
    )
    return pl.pallas_call(
        functools.partial(_sb_sample_kernel, n_heads=n_heads, n_pages=n_pages, group=group),
        grid_spec=grid_spec,
        out_shape=jax.ShapeDtypeStruct((n_seq * n_new, width), F32),
        compiler_params=pltpu.CompilerParams(
            dimension_semantics=("parallel", "arbitrary"), vmem_limit_bytes=VMEM_LIMIT),
        name="sb_sample",
    )(page_table.reshape(-1), z, z, z, *([kp] * group), *([vp] * group), bias_col,
      _strict_lower_ones(page))


def _ln_swish(c, g, b):
    mu = jnp.mean(c, axis=-1, keepdims=True)
    d = c - mu
    var = jnp.mean(d * d, axis=-1, keepdims=True)
    y = d * lax.rsqrt(var + EPS) * g + b
    return y * _sigmoid(y)


def _conv_prompt_kernel(a_ref, b_ref, w_ref, cb_ref, g_ref, lb_ref, o_ref, hist_ref,
                        u_ref, c_ref):
    tm, width = a_ref.shape

    @pl.when(pl.program_id(0) == 0)
    def _():
        u_ref[0:HIST_PAD, :] = jnp.zeros((HIST_PAD, width), F32)

    u_ref[HIST_PAD:HIST_PAD + tm, :] = a_ref[...] * _sigmoid(b_ref[...])

    def row_chunk(r, carry):
        base = pl.multiple_of(r * CONV_ROWS, CONV_ROWS)
        for lc in range(width // CONV_LANES):
            lanes = slice(lc * CONV_LANES, (lc + 1) * CONV_LANES)
            win = u_ref[pl.ds(base, CONV_ROWS + HIST_PAD), lanes]
            acc = jnp.broadcast_to(cb_ref[:, lanes], (CONV_ROWS, CONV_LANES))
            for r in range(SUBLANES):
                span = CONV_ROWS + HIST_PAD - (SUBLANES if r else 0)
                shifted = win[r:r + span, :]
                for k in range(CONV_KERNEL):
                    s = HIST_PAD - HIST + k
                    if s % SUBLANES == r:
                        acc = acc + shifted[s - r:s - r + CONV_ROWS, :] * w_ref[k:k + 1, lanes]
            c_ref[pl.ds(base, CONV_ROWS), lanes] = acc
        return carry

    lax.fori_loop(0, tm // CONV_ROWS, row_chunk, 0)
    o_ref[...] = _ln_swish(c_ref[...], g_ref[...], lb_ref[...])
    tail = u_ref[tm:tm + HIST_PAD, :]
    hist_ref[...] = tail
    u_ref[0:HIST_PAD, :] = tail


def _conv_prompt(z, conv_w, conv_b, ln_g, ln_b, a_off, b_off):
    t = z.shape[0]
    kk, width = conv_w.shape
    tm = CONV_TM
    vec = lambda: pl.BlockSpec((1, width), lambda i: (0, 0))
    return pl.pallas_call(
        _conv_prompt_kernel,
        grid=(t // tm,),
        in_specs=[
            pl.BlockSpec((tm, width), lambda i: (i, a_off // width)),
            pl.BlockSpec((tm, width), lambda i: (i, b_off // width)),
            pl.BlockSpec((kk, width), lambda i: (0, 0)),
            vec(), vec(), vec(),
        ],
        out_specs=[
            pl.BlockSpec((tm, width), lambda i: (i, 0)),
            pl.BlockSpec((HIST_PAD, width), lambda i: (0, 0)),
        ],
        out_shape=[
            jax.ShapeDtypeStruct((t, width), F32),
            jax.ShapeDtypeStruct((HIST_PAD, width), F32),
        ],
        scratch_shapes=[
            pltpu.VMEM((tm + HIST_PAD, width), F32),
            pltpu.VMEM((tm, width), F32),
        ],
        compiler_params=pltpu.CompilerParams(
            dimension_semantics=("arbitrary",), vmem_limit_bytes=VMEM_LIMIT),
        name="conv_prompt",
    )(z, z, conv_w, conv_b.reshape(1, width), ln_g.reshape(1, width), ln_b.reshape(1, width))


def _conv_sample_kernel(a_ref, b_ref, hist_ref, w_ref, cb_ref, g_ref, lb_ref, o_ref, nh_ref,
                        u_ref, *, n_new):
    n_seqs = hist_ref.shape[0]
    width = a_ref.shape[1]

    def one_seq(s, carry):
        rows = pl.ds(pl.multiple_of(s * n_new, n_new), n_new)
        u_ref[0:HIST, :] = hist_ref[s]
        u_ref[HIST:HIST + n_new, :] = a_ref[rows, :] * _sigmoid(b_ref[rows, :])
        acc = jnp.broadcast_to(cb_ref[...], (n_new, width))
        for k in range(CONV_KERNEL):
            acc = acc + u_ref[k:k + n_new, :] * w_ref[k:k + 1, :]
        o_ref[rows, :] = _ln_swish(acc, g_ref[...], lb_ref[...])
        nh_ref[s] = u_ref[n_new:n_new + HIST, :]
        return carry

    lax.fori_loop(0, n_seqs, one_seq, 0)


def _conv_sample(z, cache_conv, conv_w, conv_b, ln_g, ln_b, n_new, a_off, b_off):
    n_seq = cache_conv.shape[0]
    kk, width = conv_w.shape
    ns = CONV_SEQS
    vec = lambda: pl.BlockSpec((1, width), lambda i: (0, 0))
    return pl.pallas_call(
        functools.partial(_conv_sample_kernel, n_new=n_new),
        grid=(n_seq // ns,),
        in_specs=[
            pl.BlockSpec((ns * n_new, width), lambda i: (i, a_off // width)),
            pl.BlockSpec((ns * n_new, width), lambda i: (i, b_off // width)),
            pl.BlockSpec((ns, HIST, width), lambda i: (i, 0, 0)),
            pl.BlockSpec((kk, width), lambda i: (0, 0)),
            vec(), vec(), vec(),
        ],
        out_specs=[
            pl.BlockSpec((ns * n_new, width), lambda i: (i, 0)),
            pl.BlockSpec((ns, HIST, width), lambda i: (i, 0, 0)),
        ],
        out_shape=[
            jax.ShapeDtypeStruct((n_seq * n_new, width), F32),
            jax.ShapeDtypeStruct((n_seq, HIST, width), F32),
        ],
        scratch_shapes=[pltpu.VMEM((HIST + n_new + 2, width), F32)],
        compiler_params=pltpu.CompilerParams(
            dimension_semantics=("parallel",), vmem_limit_bytes=VMEM_LIMIT),
        name="conv_sample",
    )(z, z, cache_conv, conv_w, conv_b.reshape(1, width), ln_g.reshape(1, width),
      ln_b.reshape(1, width))


def _mem_attn_kernel(q_ref, mk_ref, mv_ref, o_ref, *, head_axis):
    width = q_ref.shape[1]
    hd = width // MEM_HEADS
    lane_tiles = hd // LANES

    n_sub = mk_ref.shape[0]
    tq = q_ref.shape[0] // n_sub

    def head_rows(ref, b, h):
        if not head_axis:
            return ref[b, :, h * hd:(h + 1) * hd]
        m = ref.shape[1] // (lane_tiles * MEM_HEADS)
        tiles = [ref[b, pl.ds(t * MEM_HEADS + h, m, stride=lane_tiles * MEM_HEADS), :]
                 for t in range(lane_tiles)]
        return jnp.concatenate(tiles, axis=1)

    units = [(b, h) for b in range(n_sub) for h in range(MEM_HEADS)]
    q_all = q_ref[...].astype(F32) * (hd ** -0.5)
    scores = []
    for b, h in units:
        q = q_all[b * tq:(b + 1) * tq, h * hd:(h + 1) * hd].astype(BF16)
        scores.append(_nt_dot(q, head_rows(mk_ref, b, h).astype(BF16)))
    probs = []
    for s in scores:
        e = jnp.exp(s - jnp.max(s, axis=-1, keepdims=True))
        probs.append((e / jnp.sum(e, axis=-1, keepdims=True)).astype(BF16))
    for (b, h), p in zip(units, probs):
        o_ref[b * tq:(b + 1) * tq, h * hd:(h + 1) * hd] = _dot(p, head_rows(mv_ref, b, h).astype(BF16))


def _mem_attn(z, mem_k, mem_v, tq, q_off):
    head_axis = mem_k.ndim == 4
    n_b, m = mem_k.shape[:2]
    width = mem_k.shape[2] * mem_k.shape[3] if head_axis else mem_k.shape[2]
    nt = z.shape[0] // (n_b * tq)
    n_sub = max(1, (2 * SUBLANES) // tq) if nt == 1 else 1
    assert n_b % n_sub == 0
    if head_axis:
        def rows_view(x):
            x = x.reshape(n_b, m, MEM_HEADS, width // MEM_HEADS // LANES, LANES)
            return jnp.transpose(x, (0, 1, 3, 2, 4)).reshape(n_b, m * width // LANES, LANES)
        mem_k, mem_v = rows_view(mem_k), rows_view(mem_v)
        mem_spec = pl.BlockSpec((n_sub, m * width // LANES, LANES), lambda b, t: (b, 0, 0))
    else:
        mem_spec = pl.BlockSpec((n_sub, m, width), lambda b, t: (b, 0, 0))
    return pl.pallas_call(
        functools.partial(_mem_attn_kernel, head_axis=head_axis),
        grid=(n_b // n_sub, nt),
        in_specs=[
            pl.BlockSpec((n_sub * tq, width), lambda b, t: (b * nt + t, q_off // width)),
            mem_spec, mem_spec,
        ],
        out_specs=pl.BlockSpec((n_sub * tq, width), lambda b, t: (b * nt + t, 0)),
        out_shape=jax.ShapeDtypeStruct((z.shape[0], width), F32),
        compiler_params=pltpu.CompilerParams(
            dimension_semantics=("parallel", "parallel"), vmem_limit_bytes=VMEM_LIMIT),
        name="mem_attn",
    )(z, mem_k, mem_v)


def _merge_kernel(osb_ref, ocv_ref, om_ref, gsb_ref, gcv_ref, gm_ref, msb_ref, mcv_ref, mm_ref,
                  mb_ref, wsb_ref, wcv_ref, wm_ref, o_ref, a_ref):
    @pl.when(pl.program_id(1) == 0)
    def _():
        for n, (o, g) in enumerate(((osb_ref, gsb_ref), (ocv_ref, gcv_ref), (om_ref, gm_ref))):
            gate = g[...].astype(F32)
            a_ref[n] = (o[...] * (gate * _sigmoid(gate))).astype(BF16)

    def mix(m_ref, n):
        return _sigmoid(m_ref[...].astype(F32) + mb_ref[n:n + 1, :])

    merged = mix(msb_ref, 0) * _dot(a_ref[0], wsb_ref[...])
    merged += mix(mcv_ref, 1) * _dot(a_ref[1], wcv_ref[...])
    merged += mix(mm_ref, 2) * _dot(a_ref[2], wm_ref[...])
    o_ref[...] = merged.astype(o_ref.dtype)


def _merge(gate_srcs, mix_src, o_sb, o_cv, o_m, merge_bias, w_sb, w_cv, w_m, g_offs, m_offs):
    rows, width = o_sb.shape
    d = w_sb.shape[1]
    tm, tn = min(MERGE_TM, rows), MERGE_TN
    branch = lambda: pl.BlockSpec((tm, width), lambda i, j: (i, 0))
    gate = lambda off: pl.BlockSpec((tm, width), lambda i, j: (i, off // width))
    mix = lambda off: pl.BlockSpec((tm, tn), lambda i, j: (i, off // tn + j))
    weight = lambda: pl.BlockSpec((width, tn), lambda i, j: (0, j))
    return pl.pallas_call(
        _merge_kernel,
        grid=(rows // tm, d // tn),
        in_specs=[branch(), branch(), branch(),
                  gate(g_offs[0]), gate(g_offs[1]), gate(g_offs[2]),
                  mix(m_offs[0]), mix(m_offs[1]), mix(m_offs[2]),
                  pl.BlockSpec((3, tn), lambda i, j: (0, j)),
                  weight(), weight(), weight()],
        out_specs=pl.BlockSpec((tm, tn), lambda i, j: (i, j)),
        out_shape=jax.ShapeDtypeStruct((rows, d), BF16),
        scratch_shapes=[pltpu.VMEM((3, tm, width), BF16)],
        compiler_params=pltpu.CompilerParams(
            dimension_semantics=("parallel", "arbitrary"), vmem_limit_bytes=VMEM_LIMIT),
        name="merge",
    )(o_sb, o_cv, o_m, *gate_srcs, mix_src, mix_src, mix_src, merge_bias, w_sb, w_cv, w_m)


def _out_kernel(x_ref, m_ref, w_ref, g_ref, o_ref, *, final_norm):
    y = x_ref[...] + _dot(m_ref[...], w_ref[...])
    if final_norm:
        r = lax.rsqrt(jnp.mean(y * y, axis=-1, keepdims=True) + EPS)
        y = y * r * g_ref[...]
    o_ref[...] = y


def _out_proj(x, merged, w_out, norm_final, final_norm):
    rows, d = x.shape
    tm = min(OUT_TM, rows)
    return pl.pallas_call(
        functools.partial(_out_kernel, final_norm=final_norm),
        grid=(rows // tm,),
        in_specs=[
            pl.BlockSpec((tm, d), lambda i: (i, 0)),
            pl.BlockSpec((tm, d), lambda i: (i, 0)),
            pl.BlockSpec((d, d), lambda i: (0, 0)),
            pl.BlockSpec((1, d), lambda i: (0, 0)),
        ],
        out_specs=pl.BlockSpec((tm, d), lambda i: (i, 0)),
        out_shape=jax.ShapeDtypeStruct((rows, d), F32),
        compiler_params=pltpu.CompilerParams(
            dimension_semantics=("parallel",), vmem_limit_bytes=VMEM_LIMIT),
        name="out_proj",
    )(x, merged, w_out, norm_final.reshape(1, d))


def kernel(x_prompt, x_sample, mem_prompt, cache_k, cache_v, cache_conv, cache_mem_k, cache_mem_v, page_table, norm_in, w_in, sb_bias, merge_bias, conv_w, conv_b, ln_g, ln_b, norm_mem, w_mem_kv, w_proj_sb, w_proj_cv, w_proj_mem, w_out, norm_final):
    bp, seq, d = x_prompt.shape
    bd, dec_seq, _ = x_sample.shape
    depth = w_in.shape[0]
    sb_w = cache_k.shape[3] * cache_k.shape[4]
    cv_w = conv_w.shape[2]
    mem_w = cache_mem_k.shape[3] * cache_mem_k.shape[4]
    mem_tokens = mem_prompt.shape[1]
    in_cols = w_in.shape[2]
    assert bp == 1 and in_cols == 4 * sb_w + 3 * cv_w + 2 * mem_w + 3 * d
    sizes = [sb_w] * 4 + [cv_w] * 3 + [mem_w] * 2 + [d] * 3
    (q_o, k_o, v_o, gsb_o, ga_o, gb_o, gcv_o, qm_o, gm_o, msb_o, mcv_o, mm_o) = (
        np.concatenate([[0], np.cumsum(sizes)[:-1]]).tolist())

    yp = x_prompt.reshape(seq, d)
    ys = x_sample.reshape(bd * dec_seq, d)
    outs = [[] for _ in range(8)]
    for l in range(depth):
        last = l == depth - 1
        w_in_l = w_in[l].astype(BF16)
        w_kv_l = w_mem_kv[l].astype(BF16)
        w_sb_l = w_proj_sb[l].astype(BF16)
        w_cv_l = w_proj_cv[l].astype(BF16)
        w_m_l = w_proj_mem[l].astype(BF16)
        w_out_l = w_out[l].astype(BF16)

        split = gcv_o

        def in_proj(x, name):
            zf = _rms_matmul(x, norm_in[l], w_in_l, 0, split, IN_TM, IN_TN, name + "_f32")
            zb = _rms_matmul(x, norm_in[l], w_in_l, split, in_cols - split, IN_TM, IN_TN,
                             name + "_bf16", BF16)
            return zf, zb

        def tail(x, zf, zb, o_sb, o_cv, o_m):
            merged = _merge((zf, zb, zb), zb, o_sb, o_cv, o_m, merge_bias[l], w_sb_l, w_cv_l, w_m_l,
                            (gsb_o, gcv_o - split, gm_o - split),
                            (msb_o - split, mcv_o - split, mm_o - split))
            return _out_proj(x, merged, w_out_l, norm_final, last)

        mem = mem_prompt.reshape(mem_tokens, d)
        mk = _rms_matmul(mem, norm_mem[l], w_kv_l, 0, mem_w, IN_TM, mem_w, "mem_k")
        mv = _rms_matmul(mem, norm_mem[l], w_kv_l, mem_w, mem_w, IN_TM, mem_w, "mem_v")
        zp, zpb = in_proj(yp, "in_proj_prompt")
        o_sb = _sb_prompt(zp, sb_bias[l], sb_w, q_o, k_o, v_o)
        o_cv, hist_p = _conv_prompt(zp, conv_w[l], conv_b[l], ln_g[l], ln_b[l], ga_o, gb_o)
        o_m = _mem_attn(zpb, mk.reshape(1, mem_tokens, mem_w), mv.reshape(1, mem_tokens, mem_w),
                        MEM_TQ, qm_o - split)
        yp_new = tail(yp, zp, zpb, o_sb, o_cv, o_m)
        heads = cache_k.shape[3:]
        outs[0].append(zp[:, k_o:k_o + sb_w].reshape(bp, seq, *heads))
        outs[1].append(zp[:, v_o:v_o + sb_w].reshape(bp, seq, *heads))
        outs[2].append(hist_p[HIST_PAD - HIST:].reshape(bp, HIST, cv_w))
        outs[3].append(mk.reshape(bp, mem_tokens, *cache_mem_k.shape[3:]))
        outs[4].append(mv.reshape(bp, mem_tokens, *cache_mem_k.shape[3:]))
        yp = yp_new

        zs, zsb = in_proj(ys, "in_proj_sample")
        o_sb = _sb_sample(zs, cache_k[l], cache_v[l], page_table, sb_bias[l], dec_seq, sb_w,
                          q_o, k_o, v_o)
        o_cv, hist_s = _conv_sample(zs, cache_conv[l], conv_w[l], conv_b[l], ln_g[l], ln_b[l],
                                    dec_seq, ga_o, gb_o)
        o_m = _mem_attn(zsb, cache_mem_k[l], cache_mem_v[l], dec_seq, qm_o - split)
        ys_new = tail(ys, zs, zsb, o_sb, o_cv, o_m)
        outs[5].append(zs[:, k_o:k_o + sb_w].reshape(bd, dec_seq, *heads))
        outs[6].append(zs[:, v_o:v_o + sb_w].reshape(bd, dec_seq, *heads))
        outs[7].append(hist_s)
        ys = ys_new

    stacked = [jnp.stack(o) for o in outs]
    return (yp.reshape(bp, seq, d), ys.reshape(bd, dec_seq, d), *stacked)
```
